```python
import math
import jax, jax.numpy as jnp
from jax import lax
import numpy as np

D_MODEL = 1024
BATCH = 32
SEQ = 2048
DEPTH = 1

HEAD_DIM = 64
SB_HEADS = (D_MODEL // 2) // HEAD_DIM
DIFF_HEADS = (D_MODEL // 2) // (2 * HEAD_DIM)
SB_WIDTH = SB_HEADS * HEAD_DIM
DIFF_WIDTH = DIFF_HEADS * 2 * HEAD_DIM
MIX_WIDTH = SB_WIDTH + DIFF_WIDTH
IN_WIDTH = 3 * MIX_WIDTH
D_FF = -(-8 * D_MODEL // (3 * 256)) * 256
Q_BLOCK = 128
ROPE_THETA = 10000.0
NORM_EPS = 1e-6
SUBLN_EPS = 1e-5

kernel_name = "hybrid_stickbreak_diffattn_swiglu"


def rmsnorm(x, g, eps=NORM_EPS):
    xf = x.astype(jnp.float32)
    y = xf * lax.rsqrt(jnp.mean(xf * xf, axis=-1, keepdims=True) + eps) * g.astype(jnp.float32)
    return y.astype(x.dtype)


def rope(x, pos):
    half = HEAD_DIM // 2
    inv_freq = ROPE_THETA ** (-jnp.arange(half, dtype=jnp.float32) / half)
    ang = pos.astype(jnp.float32)[:, None] * inv_freq[None, :]
    cos, sin = jnp.cos(ang), jnp.sin(ang)
    xf = x.astype(jnp.float32)
    x1, x2 = xf[..., :half], xf[..., half:]
    return jnp.concatenate([x1 * cos - x2 * sin, x1 * sin + x2 * cos], axis=-1).astype(x.dtype)


def stick_breaking_attention(q, k, v):
    S = q.shape[2]
    scale = HEAD_DIM ** -0.5
    outs = []
    for start in range(0, S, Q_BLOCK):
        end = start + Q_BLOCK
        z = jnp.einsum('bhqd,bhkd->bhqk', q[:, :, start:end], k[:, :, :end]).astype(jnp.float32) * scale
        past = jnp.arange(end)[None, :] < jnp.arange(start, end)[:, None]
        log_beta = jax.nn.log_sigmoid(z)
        log_1m = jnp.where(past, jax.nn.log_sigmoid(-z), 0.0)
        acc = lax.cumsum(log_1m, axis=3, reverse=True) - log_1m
        w = jnp.where(past, jnp.exp(log_beta + acc), 0.0)
        outs.append(jnp.einsum('bhqk,bhkd->bhqd', w.astype(v.dtype), v[:, :, :end]))
    return jnp.concatenate(outs, axis=2)


def differential_attention(q, k, v, lam):
    S = q.shape[3]
    scale = HEAD_DIM ** -0.5
    outs = []
    for start in range(0, S, Q_BLOCK):
        end = start + Q_BLOCK
        s = jnp.einsum('bhcqd,bhckd->bhcqk', q[:, :, :, start:end], k[:, :, :, :end]).astype(jnp.float32) * scale
        causal = jnp.arange(end)[None, :] <= jnp.arange(start, end)[:, None]
        p = jax.nn.softmax(jnp.where(causal, s, -jnp.inf), axis=-1)
        a = p[:, :, 0] - lam * p[:, :, 1]
        outs.append(jnp.einsum('bhqk,bhkd->bhqd', a.astype(v.dtype), v[:, :, :end]))
    return jnp.concatenate(outs, axis=2)


def setup_inputs(seed: int = 0) -> dict:
    key = jax.random.key(seed)
    ks = jax.random.split(key, 16)
    f32 = jnp.float32
    nrm = lambda k, shape, s: jax.random.normal(k, shape, f32) * s
    return {
        "x": nrm(ks[0], (BATCH, SEQ, D_MODEL), 1.0),
        "attn_norm_g": 1.0 + nrm(ks[1], (DEPTH, D_MODEL), 0.02),
        "w_in": nrm(ks[2], (DEPTH, D_MODEL, IN_WIDTH), D_MODEL ** -0.5),
        "diff_q_norm_g": 1.0 + nrm(ks[3], (DEPTH, HEAD_DIM), 0.02),
        "diff_k_norm_g": 1.0 + nrm(ks[4], (DEPTH, HEAD_DIM), 0.02),
        "lambda_q1": nrm(ks[5], (DEPTH, HEAD_DIM), 0.1),
        "lambda_k1": nrm(ks[6], (DEPTH, HEAD_DIM), 0.1),
        "lambda_q2": nrm(ks[7], (DEPTH, HEAD_DIM), 0.1),
        "lambda_k2": nrm(ks[8], (DEPTH, HEAD_DIM), 0.1),
        "diff_subln_g": 1.0 + nrm(ks[9], (DEPTH, 2 * HEAD_DIM), 0.02),
        "w_o": nrm(ks[10], (DEPTH, MIX_WIDTH, D_MODEL), MIX_WIDTH ** -0.5),
        "ffn_norm_g": 1.0 + nrm(ks[11], (DEPTH, D_MODEL), 0.02),
        "w_gate": nrm(ks[12], (DEPTH, D_MODEL, D_FF), D_MODEL ** -0.5),
        "w_up": nrm(ks[13], (DEPTH, D_MODEL, D_FF), D_MODEL ** -0.5),
        "w_down": nrm(ks[14], (DEPTH, D_FF, D_MODEL), D_FF ** -0.5),
    }


def reference(x, attn_norm_g, w_in, diff_q_norm_g, diff_k_norm_g, lambda_q1, lambda_k1,
              lambda_q2, lambda_k2, diff_subln_g, w_o, ffn_norm_g, w_gate, w_up, w_down):
    B, S, _ = x.shape
    pos = jnp.arange(S, dtype=jnp.int32)
    for layer in range(DEPTH):
        lambda_init = 0.8 - 0.6 * math.exp(-0.3 * layer)
        h = rmsnorm(x, attn_norm_g[layer])
        proj = h @ w_in[layer]
        sb_q, sb_k, sb_v, d_q, d_k, d_v = jnp.split(
            proj, [SB_WIDTH, 2 * SB_WIDTH, 3 * SB_WIDTH,
                   3 * SB_WIDTH + DIFF_WIDTH, 3 * SB_WIDTH + 2 * DIFF_WIDTH], axis=-1)

        to_heads = lambda t: t.reshape(B, S, SB_HEADS, HEAD_DIM).transpose(0, 2, 1, 3)
        sb_out = stick_breaking_attention(to_heads(sb_q), to_heads(sb_k), to_heads(sb_v))
        sb_out = sb_out.transpose(0, 2, 1, 3).reshape(B, S, SB_WIDTH)

        to_pair = lambda t: t.reshape(B, S, DIFF_HEADS, 2, HEAD_DIM).transpose(0, 2, 3, 1, 4)
        dq = rope(rmsnorm(to_pair(d_q), diff_q_norm_g[layer]), pos)
        dk = rope(rmsnorm(to_pair(d_k), diff_k_norm_g[layer]), pos)
        dv = d_v.reshape(B, S, DIFF_HEADS, 2 * HEAD_DIM).transpose(0, 2, 1, 3)
        lam = (jnp.exp(jnp.sum(lambda_q1[layer].astype(jnp.float32) * lambda_k1[layer].astype(jnp.float32)))
               - jnp.exp(jnp.sum(lambda_q2[layer].astype(jnp.float32) * lambda_k2[layer].astype(jnp.float32)))
               + lambda_init)
        d_out = differential_attention(dq, dk, dv, lam)
        d_out = rmsnorm(d_out, diff_subln_g[layer], SUBLN_EPS) * (1.0 - lambda_init)
        d_out = d_out.transpose(0, 2, 1, 3).reshape(B, S, DIFF_WIDTH)

        mix = jnp.concatenate([sb_out, d_out.astype(sb_out.dtype)], axis=-1)
        x = x + mix @ w_o[layer]

        h2 = rmsnorm(x, ffn_norm_g[layer])
        x = x + (jax.nn.silu(h2 @ w_gate[layer]) * (h2 @ w_up[layer])) @ w_down[layer]
    return x
```

```python
import functools
import math

import jax
import jax.numpy as jnp
from jax import lax
from jax.experimental import pallas as pl
from jax.experimental.pallas import tpu as pltpu

HEAD_DIM = 64
LANES = 128
ROPE_THETA = 10000.0
NORM_EPS = 1e-6
SUBLN_EPS = 1e-5
QK_SCALE = HEAD_DIM ** -0.5
NEG_BIG = -1e30
VMEM_LIMIT = 56 * 1024 * 1024

F32 = jnp.float32
BF16 = jnp.bfloat16


def _dot(a, b):
    return jnp.dot(a, b, preferred_element_type=F32)


def _dot_nt(a, b):
    return lax.dot_general(a, b, (((1,), (1,)), ((), ())), preferred_element_type=F32)


def _inproj_kernel(x_ref, g_ref, w_ref, gq_ref, gk_ref, cos_ref, sin_ref, o_ref, *, sb_width, diff_width):
    x = x_ref[...]
    ms = jnp.mean(x * x, axis=-1, keepdims=True)
    h = (x * lax.rsqrt(ms + NORM_EPS) * g_ref[...]).astype(BF16)

    tm = x.shape[0]
    lane = lax.broadcasted_iota(jnp.int32, (tm, LANES), 1)
    low_head = lane < HEAD_DIM
    first_half = (lane & (HEAD_DIM - 1)) < (HEAD_DIM // 2)
    cos = cos_ref[...]
    sin = sin_ref[...]

    def norm_rope(p, g, scale):
        sq = p * p
        s_lo = jnp.sum(jnp.where(low_head, sq, 0.0), axis=-1, keepdims=True)
        s_hi = jnp.sum(jnp.where(low_head, 0.0, sq), axis=-1, keepdims=True)
        ms2 = jnp.where(low_head, s_lo, s_hi) * (1.0 / HEAD_DIM)
        pn = p * lax.rsqrt(ms2 + NORM_EPS) * g
        swapped = jnp.where(first_half, pltpu.roll(pn, LANES - HEAD_DIM // 2, 1),
                            pltpu.roll(pn, HEAD_DIM // 2, 1))
        out = pn * cos + swapped * sin
        if scale != 1.0:
            out = out * scale
        return out.astype(BF16)

    q_lo, k_lo, v_lo = 0, sb_width, 2 * sb_width
    dq_lo = 3 * sb_width
    dk_lo = dq_lo + diff_width
    dv_lo = dk_lo + diff_width
    n_out = dv_lo + diff_width
    chunk = 512
    for c0 in range(0, n_out, chunk):
        p = _dot(h, w_ref[:, c0:c0 + chunk])
        if c0 < k_lo:
            o_ref[:, c0:c0 + chunk] = (p * QK_SCALE).astype(BF16)
        elif dq_lo <= c0 < dv_lo:
            is_q = c0 < dk_lo
            g = gq_ref[...] if is_q else gk_ref[...]
            for l0 in range(0, chunk, LANES):
                o_ref[:, c0 + l0:c0 + l0 + LANES] = norm_rope(
                    p[:, l0:l0 + LANES], g, QK_SCALE if is_q else 1.0)
        else:
            o_ref[:, c0:c0 + chunk] = p.astype(BF16)


def _inproj(x2, g, w_bf, gq_t, gk_t, cos_t, sin_t, *, seq, sb_width, diff_width, tm):
    t, d = x2.shape
    n_out = w_bf.shape[1]
    n_s = seq // tm
    kern = functools.partial(_inproj_kernel, sb_width=sb_width, diff_width=diff_width)
    return pl.pallas_call(
        kern,
        grid=(t // tm,),
        in_specs=[
            pl.BlockSpec((tm, d), lambda i: (i, 0)),
            pl.BlockSpec((1, d), lambda i: (0, 0)),
            pl.BlockSpec((d, n_out), lambda i: (0, 0)),
            pl.BlockSpec((1, LANES), lambda i: (0, 0)),
            pl.BlockSpec((1, LANES), lambda i: (0, 0)),
            pl.BlockSpec((tm, LANES), lambda i: (i % n_s, 0)),
            pl.BlockSpec((tm, LANES), lambda i: (i % n_s, 0)),
        ],
        out_specs=pl.BlockSpec((tm, n_out), lambda i: (i, 0)),
        out_shape=jax.ShapeDtypeStruct((t, n_out), BF16),
        compiler_params=pltpu.CompilerParams(
            dimension_semantics=("arbitrary",), vmem_limit_bytes=VMEM_LIMIT),
        name="inproj",
    )(x2, g, w_bf, gq_t, gk_t, cos_t, sin_t)


def _sb_kernel(q_ref, k_ref, v_ref, o_ref, km_ref, *, tq):
    seq = q_ref.shape[0]
    nq = seq // tq
    lane_k = lax.broadcasted_iota(jnp.int32, (seq, LANES), 1)
    k_all = k_ref[...]
    zero = jnp.zeros_like(k_all)
    km_ref[0] = jnp.where(lane_k < HEAD_DIM, k_all, zero)
    km_ref[1] = jnp.where(lane_k < HEAD_DIM, zero, k_all)

    row = lax.broadcasted_iota(jnp.int32, (tq, tq), 0)
    col = lax.broadcasted_iota(jnp.int32, (tq, tq), 1)
    past = col < row
    suffix = (row > col).astype(BF16)
    lane_o = lax.broadcasted_iota(jnp.int32, (tq, LANES), 1)

    def tile(q, kb, vb, carry, diag):
        rowsum, acc = carry
        s = _dot_nt(q, kb)
        lb = jnp.minimum(s, 0.0) - jnp.log(1.0 + jnp.exp(-jnp.abs(s)))
        l1m = lb - s
        if diag:
            l1m = jnp.where(past, l1m, 0.0)
        hi = l1m.astype(BF16)
        lo = (l1m - hi.astype(F32)).astype(BF16)
        cum = _dot(hi, suffix) + _dot(lo, suffix)
        w = jnp.exp(lb + cum + rowsum)
        if diag:
            w = jnp.where(past, w, 0.0)
        acc = acc + _dot(w.astype(BF16), vb)
        rowsum = rowsum + jnp.sum(l1m, axis=-1, keepdims=True)
        return rowsum, acc

    def q_block(i, _):
        r0 = pl.multiple_of(i * tq, tq)
        q = q_ref[pl.ds(r0, tq), :]
        outs = []
        for h in range(2):
            init = (jnp.zeros((tq, 1), F32), jnp.zeros((tq, LANES), F32))
            carry = tile(q, km_ref[h, pl.ds(r0, tq), :], v_ref[pl.ds(r0, tq), :], init, True)

            def kv_step(jj, c, h=h):
                c0 = pl.multiple_of((i - 1 - jj) * tq, tq)
                return tile(q, km_ref[h, pl.ds(c0, tq), :], v_ref[pl.ds(c0, tq), :], c, False)

            carry = lax.fori_loop(0, i, kv_step, carry)
            outs.append(carry[1])
        o_ref[pl.ds(r0, tq), :] = jnp.where(lane_o < HEAD_DIM, outs[0], outs[1]).astype(o_ref.dtype)
        return 0

    lax.fori_loop(0, nq, q_block, 0)


def _sb_attention(proj, *, n_pairs, q_blk, k_blk, v_blk, tq):
    b, seq, _ = proj.shape
    kern = functools.partial(_sb_kernel, tq=tq)
    spec = lambda off: pl.BlockSpec((None, seq, LANES), lambda bi, hp: (bi, 0, off + hp))
    return pl.pallas_call(
        kern,
        grid=(b, n_pairs),
        in_specs=[spec(q_blk), spec(k_blk), spec(v_blk)],
        out_specs=pl.BlockSpec((None, seq, LANES), lambda bi, hp: (bi, 0, hp)),
        out_shape=jax.ShapeDtypeStruct((b, seq, n_pairs * LANES), BF16),
        scratch_shapes=[pltpu.VMEM((2, seq, LANES), BF16)],
        compiler_params=pltpu.CompilerParams(
            dimension_semantics=("arbitrary", "arbitrary"), vmem_limit_bytes=VMEM_LIMIT),
        name="sb_attn",
    )(proj, proj, proj)


def _diff_kernel(q_ref, k_ref, v_ref, lq1_ref, lk1_ref, lq2_ref, lk2_ref, gs_ref, o_ref, km_ref,
                 *, tq, lambda_init):
    seq = q_ref.shape[0]
    nq = seq // tq
    lane_k = lax.broadcasted_iota(jnp.int32, (seq, LANES), 1)
    k_all = k_ref[...]
    zero = jnp.zeros_like(k_all)
    km_ref[0] = jnp.where(lane_k < HEAD_DIM, k_all, zero)
    km_ref[1] = jnp.where(lane_k < HEAD_DIM, zero, k_all)

    lam = (jnp.exp(jnp.sum(lq1_ref[...] * lk1_ref[...], axis=-1, keepdims=True))
           - jnp.exp(jnp.sum(lq2_ref[...] * lk2_ref[...], axis=-1, keepdims=True))
           + lambda_init)

    row = lax.broadcasted_iota(jnp.int32, (tq, tq), 0)
    col = lax.broadcasted_iota(jnp.int32, (tq, tq), 1)
    causal = col <= row

    def tile(q, kb, vb, carry, diag):
        m, l, acc = carry
        s = _dot_nt(q, kb)
        if diag:
            s = jnp.where(causal, s, NEG_BIG)
        m_new = jnp.maximum(m, jnp.max(s, axis=-1, keepdims=True))
        alpha = jnp.exp(m - m_new)
        p = jnp.exp(s - m_new)
        l = alpha * l + jnp.sum(p, axis=-1, keepdims=True)
        acc = alpha * acc + _dot(p.astype(BF16), vb)
        return m_new, l, acc

    def q_block(i, _):
        r0 = pl.multiple_of(i * tq, tq)
        q = q_ref[pl.ds(r0, tq), :]
        res = []
        for c in range(2):
            init = (jnp.full((tq, 1), NEG_BIG, F32), jnp.zeros((tq, 1), F32),
                    jnp.zeros((tq, LANES), F32))
            carry = tile(q, km_ref[c, pl.ds(r0, tq), :], v_ref[pl.ds(r0, tq), :], init, True)

            def kv_step(jj, cr, c=c):
                c0 = pl.multiple_of((i - 1 - jj) * tq, tq)
                return tile(q, km_ref[c, pl.ds(c0, tq), :], v_ref[pl.ds(c0, tq), :], cr, False)

            _, l, acc = lax.fori_loop(0, i, kv_step, carry)
            res.append(acc / l)
        o = res[0] - lam * res[1]
        ms = jnp.mean(o * o, axis=-1, keepdims=True)
        o = o * lax.rsqrt(ms + SUBLN_EPS) * gs_ref[...]
        o_ref[pl.ds(r0, tq), :] = (o * (1.0 - lambda_init)).astype(o_ref.dtype)
        return 0

    lax.fori_loop(0, nq, q_block, 0)


def _diff_attention(proj, lq1, lk1, lq2, lk2, gs, *, n_heads, q_blk, k_blk, v_blk, tq, lambda_init):
    b, seq, _ = proj.shape
    kern = functools.partial(_diff_kernel, tq=tq, lambda_init=lambda_init)
    spec = lambda off: pl.BlockSpec((None, seq, LANES), lambda bi, h: (bi, 0, off + h))
    small = lambda n: pl.BlockSpec((1, n), lambda bi, h: (0, 0))
    return pl.pallas_call(
        kern,
        grid=(b, n_heads),
        in_specs=[spec(q_blk), spec(k_blk), spec(v_blk),
                  small(HEAD_DIM), small(HEAD_DIM), small(HEAD_DIM), small(HEAD_DIM), small(LANES)],
        out_specs=pl.BlockSpec((None, seq, LANES), lambda bi, h: (bi, 0, h)),
        out_shape=jax.ShapeDtypeStruct((b, seq, n_heads * LANES), BF16),
        scratch_shapes=[pltpu.VMEM((2, seq, LANES), BF16)],
        compiler_params=pltpu.CompilerParams(
            dimension_semantics=("arbitrary", "arbitrary"), vmem_limit_bytes=VMEM_LIMIT),
        name="diff_attn",
    )(proj, proj, proj, lq1, lk1, lq2, lk2, gs)


def _out_ffn_kernel(x_ref, sb_ref, df_ref, wo_sb_ref, wo_df_ref, g_ref, wg_ref, wu_ref, wd_ref, o_ref):
    x1 = x_ref[...] + _dot(sb_ref[...], wo_sb_ref[...]) + _dot(df_ref[...], wo_df_ref[...])
    ms = jnp.mean(x1 * x1, axis=-1, keepdims=True)
    h2 = (x1 * lax.rsqrt(ms + NORM_EPS) * g_ref[...]).astype(BF16)
    gate = _dot(h2, wg_ref[...])
    up = _dot(h2, wu_ref[...])
    act = (gate * jax.nn.sigmoid(gate) * up).astype(BF16)
    o_ref[...] = x1 + _dot(act, wd_ref[...])


def _out_ffn(x2, sb, df, wo_sb, wo_df, g, wg, wu, wd, *, tm):
    t, d = x2.shape
    d_ff = wg.shape[1]
    resident = lambda shape: pl.BlockSpec(shape, lambda i: (0, 0), pipeline_mode=pl.Buffered(1))
    return pl.pallas_call(
        _out_ffn_kernel,
        grid=(t // tm,),
        in_specs=[
            pl.BlockSpec((tm, d), lambda i: (i, 0)),
            pl.BlockSpec((tm, sb.shape[1]), lambda i: (i, 0)),
            pl.BlockSpec((tm, df.shape[1]), lambda i: (i, 0)),
            resident(wo_sb.shape), resident(wo_df.shape),
            pl.BlockSpec((1, d), lambda i: (0, 0)),
            resident((d, d_ff)), resident((d, d_ff)), resident((d_ff, d)),
        ],
        out_specs=pl.BlockSpec((tm, d), lambda i: (i, 0)),
        out_shape=jax.ShapeDtypeStruct((t, d), F32),
        compiler_params=pltpu.CompilerParams(
            dimension_semantics=("arbitrary",), vmem_limit_bytes=VMEM_LIMIT),
        name="out_ffn",
    )(x2, sb, df, wo_sb, wo_df, g, wg, wu, wd)


def _rope_tables(seq):
    half = HEAD_DIM // 2
    inv_freq = ROPE_THETA ** (-jnp.arange(half, dtype=F32) / half)
    ang = jnp.arange(seq, dtype=jnp.int32).astype(F32)[:, None] * inv_freq[None, :]
    cos, sin = jnp.cos(ang), jnp.sin(ang)
    reps = LANES // HEAD_DIM
    cos_t = jnp.tile(jnp.concatenate([cos, cos], axis=-1), (1, reps))
    sin_t = jnp.tile(jnp.concatenate([-sin, sin], axis=-1), (1, reps))
    return cos_t, sin_t


def kernel(x, attn_norm_g, w_in, diff_q_norm_g, diff_k_norm_g, lambda_q1, lambda_k1, lambda_q2, lambda_k2,
           diff_subln_g, w_o, ffn_norm_g, w_gate, w_up, w_down):
    b, seq, d = x.shape
    depth = w_in.shape[0]
    mix_width = w_o.shape[1]
    sb_width = mix_width // 2
    diff_width = mix_width - sb_width
    assert sb_width % LANES == 0 and diff_width % LANES == 0
    n_pairs = sb_width // LANES
    n_dheads = diff_width // LANES
    cos_t, sin_t = _rope_tables(seq)
    reps = LANES // HEAD_DIM

    x2 = x.reshape(b * seq, d)
    for layer in range(depth):
        lambda_init = 0.8 - 0.6 * math.exp(-0.3 * layer)
        proj = _inproj(
            x2, attn_norm_g[layer][None, :], w_in[layer].astype(BF16),
            jnp.tile(diff_q_norm_g[layer], reps)[None, :], jnp.tile(diff_k_norm_g[layer], reps)[None, :],
            cos_t, sin_t, seq=seq, sb_width=sb_width, diff_width=diff_width, tm=512)
        proj = proj.reshape(b, seq, -1)
        sb = _sb_attention(proj, n_pairs=n_pairs, q_blk=0, k_blk=n_pairs, v_blk=2 * n_pairs, tq=256)
        d0 = 3 * n_pairs
        df = _diff_attention(
            proj, lambda_q1[layer][None, :], lambda_k1[layer][None, :], lambda_q2[layer][None, :],
            lambda_k2[layer][None, :], diff_subln_g[layer][None, :],
            n_heads=n_dheads, q_blk=d0, k_blk=d0 + n_dheads, v_blk=d0 + 2 * n_dheads, tq=256,
            lambda_init=lambda_init)
        wo = w_o[layer].astype(BF16)
        x2 = _out_ffn(
            x2, sb.reshape(b * seq, sb_width), df.reshape(b * seq, diff_width),
            wo[:sb_width], wo[sb_width:], ffn_norm_g[layer][None, :],
            w_gate[layer].astype(BF16), w_up[layer].astype(BF16), w_down[layer].astype(BF16),
            tm=512)
    return x2.reshape(b, seq, d)
```

```python
import functools
import math

import jax
import jax.numpy as jnp
from jax import lax
from jax.experimental import pallas as pl
from jax.experimental.pallas import tpu as pltpu

HEAD_DIM = 64
LANES = 128
ROPE_THETA = 10000.0
NORM_EPS = 1e-6
SUBLN_EPS = 1e-5
QK_SCALE = HEAD_DIM ** -0.5
NEG_BIG = -1e30
VMEM_LIMIT = 56 * 1024 * 1024

F32 = jnp.float32
BF16 = jnp.bfloat16


def _dot(a, b):
    return jnp.dot(a, b, preferred_element_type=F32)


def _dot_nt(a, b):
    return lax.dot_general(a, b, (((1,), (1,)), ((), ())), preferred_element_type=F32)


def _inproj_kernel(x_ref, g_ref, w_ref, gq_ref, gk_ref, cos_ref, sin_ref, o_ref, *, sb_width, diff_width):
    x = x_ref[...]
    ms = jnp.mean(x * x, axis=-1, keepdims=True)
    h = (x * lax.rsqrt(ms + NORM_EPS) * g_ref[...]).astype(BF16)

    tm = x.shape[0]
    lane = lax.broadcasted_iota(jnp.int32, (tm, LANES), 1)
    low_head = lane < HEAD_DIM
    first_half = (lane & (HEAD_DIM - 1)) < (HEAD_DIM // 2)
    cos = cos_ref[...]
    sin = sin_ref[...]

    def norm_rope(p, g, scale):
        sq = p * p
        s_lo = jnp.sum(jnp.where(low_head, sq, 0.0), axis=-1, keepdims=True)
        s_hi = jnp.sum(jnp.where(low_head, 0.0, sq), axis=-1, keepdims=True)
        ms2 = jnp.where(low_head, s_lo, s_hi) * (1.0 / HEAD_DIM)
        pn = p * lax.rsqrt(ms2 + NORM_EPS) * g
        swapped = jnp.where(first_half, pltpu.roll(pn, LANES - HEAD_DIM // 2, 1),
                            pltpu.roll(pn, HEAD_DIM // 2, 1))
        out = pn * cos + swapped * sin
        if scale != 1.0:
            out = out * scale
        return out.astype(BF16)

    q_lo, k_lo, v_lo = 0, sb_width, 2 * sb_width
    dq_lo = 3 * sb_width
    dk_lo = dq_lo + diff_width
    dv_lo = dk_lo + diff_width
    n_out = dv_lo + diff_width
    chunk = 512
    for c0 in range(0, n_out, chunk):
        p = _dot(h, w_ref[:, c0:c0 + chunk])
        if c0 < k_lo:
            o_ref[:, c0:c0 + chunk] = (p * QK_SCALE).astype(BF16)
        elif dq_lo <= c0 < dv_lo:
            is_q = c0 < dk_lo
            g = gq_ref[...] if is_q else gk_ref[...]
            for l0 in range(0, chunk, LANES):
                o_ref[:, c0 + l0:c0 + l0 + LANES] = norm_rope(
                    p[:, l0:l0 + LANES], g, QK_SCALE if is_q else 1.0)
        else:
            o_ref[:, c0:c0 + chunk] = p.astype(BF16)


def _inproj(x2, g, w_bf, gq_t, gk_t, cos_t, sin_t, *, seq, sb_width, diff_width, tm):
    t, d = x2.shape
    n_out = w_bf.shape[1]
    n_s = seq // tm
    kern = functools.partial(_inproj_kernel, sb_width=sb_width, diff_width=diff_width)
    return pl.pallas_call(
        kern,
        grid=(t // tm,),
        in_specs=[
            pl.BlockSpec((tm, d), lambda i: (i, 0)),
            pl.BlockSpec((1, d), lambda i: (0, 0)),
            pl.BlockSpec((d, n_out), lambda i: (0, 0)),
            pl.BlockSpec((1, LANES), lambda i: (0, 0)),
            pl.BlockSpec((1, LANES), lambda i: (0, 0)),
            pl.BlockSpec((tm, LANES), lambda i: (i % n_s, 0)),
            pl.BlockSpec((tm, LANES), lambda i: (i % n_s, 0)),
        ],
        out_specs=pl.BlockSpec((tm, n_out), lambda i: (i, 0)),
        out_shape=jax.ShapeDtypeStruct((t, n_out), BF16),
        compiler_params=pltpu.CompilerParams(
            dimension_semantics=("arbitrary",), vmem_limit_bytes=VMEM_LIMIT),
        name="inproj",
    )(x2, g, w_bf, gq_t, gk_t, cos_t, sin_t)


def _masked_keys(k_ref, km_ref, groups):
    seq = k_ref.shape[0]
    lane = lax.broadcasted_iota(jnp.int32, (seq, LANES), 1)
    for g in range(groups):
        k_g = k_ref[:, g * LANES:(g + 1) * LANES]
        zero = jnp.zeros_like(k_g)
        km_ref[2 * g] = jnp.where(lane < HEAD_DIM, k_g, zero)
        km_ref[2 * g + 1] = jnp.where(lane < HEAD_DIM, zero, k_g)


def _sb_kernel(q_ref, k_ref, v_ref, o_ref, km_ref, *, tq, groups):
    seq = q_ref.shape[0]
    nq = seq // tq
    _masked_keys(k_ref, km_ref, groups)

    row = lax.broadcasted_iota(jnp.int32, (tq, tq), 0)
    col = lax.broadcasted_iota(jnp.int32, (tq, tq), 1)
    past = col < row
    suffix = (row > col).astype(BF16)
    lane_o = lax.broadcasted_iota(jnp.int32, (tq, LANES), 1)
    gsl = [slice(g * LANES, (g + 1) * LANES) for g in range(groups)]

    def q_block(i, _):
        r0 = pl.multiple_of(i * tq, tq)
        qs = [q_ref[pl.ds(r0, tq), gsl[g]] for g in range(groups)]
        heads = range(2 * groups)

        def tiles(c0, carries, diag):
            s = [_dot_nt(qs[h // 2], km_ref[h, pl.ds(c0, tq), :]) for h in heads]
            lb, l1m, hi, lo = [], [], [], []
            for h in heads:
                lb_h = jnp.minimum(s[h], 0.0) - jnp.log(1.0 + jnp.exp(-jnp.abs(s[h])))
                l1m_h = lb_h - s[h]
                if diag:
                    l1m_h = jnp.where(past, l1m_h, 0.0)
                hi_h = l1m_h.astype(BF16)
                lb.append(lb_h)
                l1m.append(l1m_h)
                hi.append(hi_h)
                lo.append((l1m_h - hi_h.astype(F32)).astype(BF16))
            cum = [_dot(hi[h], suffix) + _dot(lo[h], suffix) for h in heads]
            w = []
            for h in heads:
                w_h = jnp.exp(lb[h] + cum[h] + carries[h][0])
                if diag:
                    w_h = jnp.where(past, w_h, 0.0)
                w.append(w_h.astype(BF16))
            return tuple(
                (carries[h][0] + jnp.sum(l1m[h], axis=-1, keepdims=True),
                 carries[h][1] + _dot(w[h], v_ref[pl.ds(c0, tq), gsl[h // 2]]))
                for h in heads)

        init = tuple((jnp.zeros((tq, 1), F32), jnp.zeros((tq, LANES), F32)) for _ in range(2 * groups))
        carries = tiles(r0, init, True)

        def kv_step(jj, cs):
            return tiles(pl.multiple_of((i - 1 - jj) * tq, tq), cs, False)

        carries = lax.fori_loop(0, i, kv_step, carries)
        for g in range(groups):
            o_ref[pl.ds(r0, tq), gsl[g]] = jnp.where(
                lane_o < HEAD_DIM, carries[2 * g][1], carries[2 * g + 1][1]).astype(o_ref.dtype)
        return 0

    lax.fori_loop(0, nq, q_block, 0)


def _sb_attention(proj, *, n_pairs, q_blk, k_blk, v_blk, tq, groups):
    b, seq, _ = proj.shape
    width = groups * LANES
    steps = n_pairs // groups
    kern = functools.partial(_sb_kernel, tq=tq, groups=groups)
    spec = lambda off: pl.BlockSpec((None, seq, width), lambda bi, hp: (bi, 0, off // groups + hp))
    return pl.pallas_call(
        kern,
        grid=(b, steps),
        in_specs=[spec(q_blk), spec(k_blk), spec(v_blk)],
        out_specs=pl.BlockSpec((None, seq, width), lambda bi, hp: (bi, 0, hp)),
        out_shape=jax.ShapeDtypeStruct((b, seq, n_pairs * LANES), BF16),
        scratch_shapes=[pltpu.VMEM((2 * groups, seq, LANES), BF16)],
        compiler_params=pltpu.CompilerParams(
            dimension_semantics=("arbitrary", "arbitrary"), vmem_limit_bytes=VMEM_LIMIT),
        name="sb_attn",
    )(proj, proj, proj)


def _diff_kernel(q_ref, k_ref, v_ref, lq1_ref, lk1_ref, lq2_ref, lk2_ref, gs_ref, o_ref, km_ref,
                 *, tq, groups, lambda_init):
    seq = q_ref.shape[0]
    nq = seq // tq
    _masked_keys(k_ref, km_ref, groups)

    lam = (jnp.exp(jnp.sum(lq1_ref[...] * lk1_ref[...], axis=-1, keepdims=True))
           - jnp.exp(jnp.sum(lq2_ref[...] * lk2_ref[...], axis=-1, keepdims=True))
           + lambda_init)

    row = lax.broadcasted_iota(jnp.int32, (tq, tq), 0)
    col = lax.broadcasted_iota(jnp.int32, (tq, tq), 1)
    causal = col <= row
    gsl = [slice(g * LANES, (g + 1) * LANES) for g in range(groups)]

    def q_block(i, _):
        r0 = pl.multiple_of(i * tq, tq)
        qs = [q_ref[pl.ds(r0, tq), gsl[g]] for g in range(groups)]
        maps = range(2 * groups)

        def tiles(c0, carries, diag):
            s = [_dot_nt(qs[n // 2], km_ref[n, pl.ds(c0, tq), :]) for n in maps]
            m_new, alpha, l_new, p = [], [], [], []
            for n in maps:
                m, l, _ = carries[n]
                s_n = jnp.where(causal, s[n], NEG_BIG) if diag else s[n]
                m_n = jnp.maximum(m, jnp.max(s_n, axis=-1, keepdims=True))
                a_n = jnp.exp(m - m_n)
                p_n = jnp.exp(s_n - m_n)
                m_new.append(m_n)
                alpha.append(a_n)
                l_new.append(a_n * l + jnp.sum(p_n, axis=-1, keepdims=True))
                p.append(p_n.astype(BF16))
            return tuple(
                (m_new[n], l_new[n],
                 alpha[n] * carries[n][2] + _dot(p[n], v_ref[pl.ds(c0, tq), gsl[n // 2]]))
                for n in maps)

        init = tuple((jnp.full((tq, 1), NEG_BIG, F32), jnp.zeros((tq, 1), F32),
                      jnp.zeros((tq, LANES), F32)) for _ in range(2 * groups))
        carries = tiles(r0, init, True)

        def kv_step(jj, cs):
            return tiles(pl.multiple_of((i - 1 - jj) * tq, tq), cs, False)

        carries = lax.fori_loop(0, i, kv_step, carries)
        for g in range(groups):
            (_, l1, a1), (_, l2, a2) = carries[2 * g], carries[2 * g + 1]
            o = a1 / l1 - lam * (a2 / l2)
            ms = jnp.mean(o * o, axis=-1, keepdims=True)
            o = o * lax.rsqrt(ms + SUBLN_EPS) * gs_ref[...]
            o_ref[pl.ds(r0, tq), gsl[g]] = (o * (1.0 - lambda_init)).astype(o_ref.dtype)
        return 0

    lax.fori_loop(0, nq, q_block, 0)


def _diff_attention(proj, lq1, lk1, lq2, lk2, gs, *, n_heads, q_blk, k_blk, v_blk, tq, groups, lambda_init):
    b, seq, _ = proj.shape
    width = groups * LANES
    steps = n_heads // groups
    kern = functools.partial(_diff_kernel, tq=tq, groups=groups, lambda_init=lambda_init)
    spec = lambda off: pl.BlockSpec((None, seq, width), lambda bi, h: (bi, 0, off // groups + h))
    small = lambda n: pl.BlockSpec((1, n), lambda bi, h: (0, 0))
    return pl.pallas_call(
        kern,
        grid=(b, steps),
        in_specs=[spec(q_blk), spec(k_blk), spec(v_blk),
                  small(HEAD_DIM), small(HEAD_DIM), small(HEAD_DIM), small(HEAD_DIM), small(LANES)],
        out_specs=pl.BlockSpec((None, seq, width), lambda bi, h: (bi, 0, h)),
        out_shape=jax.ShapeDtypeStruct((b, seq, n_heads * LANES), BF16),
        scratch_shapes=[pltpu.VMEM((2 * groups, seq, LANES), BF16)],
        compiler_params=pltpu.CompilerParams(
            dimension_semantics=("arbitrary", "arbitrary"), vmem_limit_bytes=VMEM_LIMIT),
        name="diff_attn",
    )(proj, proj, proj, lq1, lk1, lq2, lk2, gs)


def _out_ffn_kernel(x_ref, sb_ref, df_ref, wo_sb_ref, wo_df_ref, g_ref, wg_ref, wu_ref, wd_ref, o_ref):
    x1 = x_ref[...] + _dot(sb_ref[...], wo_sb_ref[...]) + _dot(df_ref[...], wo_df_ref[...])
    ms = jnp.mean(x1 * x1, axis=-1, keepdims=True)
    h2 = (x1 * lax.rsqrt(ms + NORM_EPS) * g_ref[...]).astype(BF16)
    gate = _dot(h2, wg_ref[...])
    up = _dot(h2, wu_ref[...])
    act = (gate * jax.nn.sigmoid(gate) * up).astype(BF16)
    o_ref[...] = x1 + _dot(act, wd_ref[...])


def _out_ffn(x2, sb, df, wo_sb, wo_df, g, wg, wu, wd, *, tm):
    t, d = x2.shape
    d_ff = wg.shape[1]
    resident = lambda shape: pl.BlockSpec(shape, lambda i: (0, 0), pipeline_mode=pl.Buffered(1))
    return pl.pallas_call(
        _out_ffn_kernel,
        grid=(t // tm,),
        in_specs=[
            pl.BlockSpec((tm, d), lambda i: (i, 0)),
            pl.BlockSpec((tm, sb.shape[1]), lambda i: (i, 0)),
            pl.BlockSpec((tm, df.shape[1]), lambda i: (i, 0)),
            resident(wo_sb.shape), resident(wo_df.shape),
            pl.BlockSpec((1, d), lambda i: (0, 0)),
            resident((d, d_ff)), resident((d, d_ff)), resident((d_ff, d)),
        ],
        out_specs=pl.BlockSpec((tm, d), lambda i: (i, 0)),
        out_shape=jax.ShapeDtypeStruct((t, d), F32),
        compiler_params=pltpu.CompilerParams(
            dimension_semantics=("arbitrary",), vmem_limit_bytes=VMEM_LIMIT),
        name="out_ffn",
    )(x2, sb, df, wo_sb, wo_df, g, wg, wu, wd)


def _rope_tables(seq):
    half = HEAD_DIM // 2
    inv_freq = ROPE_THETA ** (-jnp.arange(half, dtype=F32) / half)
    ang = jnp.arange(seq, dtype=jnp.int32).astype(F32)[:, None] * inv_freq[None, :]
    cos, sin = jnp.cos(ang), jnp.sin(ang)
    reps = LANES // HEAD_DIM
    cos_t = jnp.tile(jnp.concatenate([cos, cos], axis=-1), (1, reps))
    sin_t = jnp.tile(jnp.concatenate([-sin, sin], axis=-1), (1, reps))
    return cos_t, sin_t


def kernel(x, attn_norm_g, w_in, diff_q_norm_g, diff_k_norm_g, lambda_q1, lambda_k1, lambda_q2, lambda_k2,
           diff_subln_g, w_o, ffn_norm_g, w_gate, w_up, w_down):
    b, seq, d = x.shape
    depth = w_in.shape[0]
    mix_width = w_o.shape[1]
    sb_width = mix_width // 2
    diff_width = mix_width - sb_width
    assert sb_width % LANES == 0 and diff_width % LANES == 0
    n_pairs = sb_width // LANES
    n_dheads = diff_width // LANES
    cos_t, sin_t = _rope_tables(seq)
    reps = LANES // HEAD_DIM

    x2 = x.reshape(b * seq, d)
    for layer in range(depth):
        lambda_init = 0.8 - 0.6 * math.exp(-0.3 * layer)
        proj = _inproj(
            x2, attn_norm_g[layer][None, :], w_in[layer].astype(BF16),
            jnp.tile(diff_q_norm_g[layer], reps)[None, :], jnp.tile(diff_k_norm_g[layer], reps)[None, :],
            cos_t, sin_t, seq=seq, sb_width=sb_width, diff_width=diff_width, tm=512)
        proj = proj.reshape(b, seq, -1)
        sb = _sb_attention(proj, n_pairs=n_pairs, q_blk=0, k_blk=n_pairs, v_blk=2 * n_pairs, tq=256, groups=2)
        d0 = 3 * n_pairs
        df = _diff_attention(
            proj, lambda_q1[layer][None, :], lambda_k1[layer][None, :], lambda_q2[layer][None, :],
            lambda_k2[layer][None, :], diff_subln_g[layer][None, :],
            n_heads=n_dheads, q_blk=d0, k_blk=d0 + n_dheads, v_blk=d0 + 2 * n_dheads, tq=256,
            groups=2, lambda_init=lambda_init)
        wo = w_o[layer].astype(BF16)
        x2 = _out_ffn(
            x2, sb.reshape(b * seq, sb_width), df.reshape(b * seq, diff_width),
            wo[:sb_width], wo[sb_width:], ffn_norm_g[layer][None, :],
            w_gate[layer].astype(BF16), w_up[layer].astype(BF16), w_down[layer].astype(BF16),
            tm=512)
    return x2.reshape(b, seq, d)
```

```python
import functools
import math

import jax
import jax.numpy as jnp
from jax import lax
from jax.experimental import pallas as pl
from jax.experimental.pallas import tpu as pltpu

HEAD_DIM = 64
LANES = 128
ROPE_THETA = 10000.0
NORM_EPS = 1e-6
SUBLN_EPS = 1e-5
LOG2E = math.log2(math.e)
Q_SCALE = HEAD_DIM ** -0.5 * LOG2E
NEG_BIG = -1e30
MAX_STATIC_SHIFT = 50.0
VMEM_LIMIT = 56 * 1024 * 1024

F32 = jnp.float32
BF16 = jnp.bfloat16


def _dot(a, b):
    return jnp.dot(a, b, preferred_element_type=F32)


def _dot_nt(a, b):
    return lax.dot_general(a, b, (((1,), (1,)), ((), ())), preferred_element_type=F32)


def _inproj_kernel(x_ref, g_ref, w_ref, gq_ref, gk_ref, cos_ref, sin_ref, o_ref, *, sb_width, diff_width):
    x = x_ref[...]
    ms = jnp.mean(x * x, axis=-1, keepdims=True)
    h = (x * lax.rsqrt(ms + NORM_EPS) * g_ref[...]).astype(BF16)

    tm = x.shape[0]
    lane = lax.broadcasted_iota(jnp.int32, (tm, LANES), 1)
    low_head = lane < HEAD_DIM
    first_half = (lane & (HEAD_DIM - 1)) < (HEAD_DIM // 2)
    cos = cos_ref[...]
    sin = sin_ref[...]

    def norm_rope(p, g, scale):
        sq = p * p
        s_lo = jnp.sum(jnp.where(low_head, sq, 0.0), axis=-1, keepdims=True)
        s_hi = jnp.sum(jnp.where(low_head, 0.0, sq), axis=-1, keepdims=True)
        ms2 = jnp.where(low_head, s_lo, s_hi) * (1.0 / HEAD_DIM)
        pn = p * lax.rsqrt(ms2 + NORM_EPS) * g
        swapped = jnp.where(first_half, pltpu.roll(pn, LANES - HEAD_DIM // 2, 1),
                            pltpu.roll(pn, HEAD_DIM // 2, 1))
        out = pn * cos + swapped * sin
        if scale != 1.0:
            out = out * scale
        return out.astype(BF16)

    q_lo, k_lo, v_lo = 0, sb_width, 2 * sb_width
    dq_lo = 3 * sb_width
    dk_lo = dq_lo + diff_width
    dv_lo = dk_lo + diff_width
    n_out = dv_lo + diff_width
    chunk = 512
    for c0 in range(0, n_out, chunk):
        p = _dot(h, w_ref[:, c0:c0 + chunk])
        if c0 < k_lo:
            o_ref[:, c0:c0 + chunk] = (p * Q_SCALE).astype(BF16)
        elif dq_lo <= c0 < dv_lo:
            is_q = c0 < dk_lo
            g = gq_ref[...] if is_q else gk_ref[...]
            for l0 in range(0, chunk, LANES):
                o_ref[:, c0 + l0:c0 + l0 + LANES] = norm_rope(
                    p[:, l0:l0 + LANES], g, Q_SCALE if is_q else 1.0)
        else:
            o_ref[:, c0:c0 + chunk] = p.astype(BF16)


def _inproj(x2, g, w_bf, gq_t, gk_t, cos_t, sin_t, *, seq, sb_width, diff_width, tm):
    t, d = x2.shape
    n_out = w_bf.shape[1]
    n_s = seq // tm
    kern = functools.partial(_inproj_kernel, sb_width=sb_width, diff_width=diff_width)
    return pl.pallas_call(
        kern,
        grid=(t // tm,),
        in_specs=[
            pl.BlockSpec((tm, d), lambda i: (i, 0)),
            pl.BlockSpec((1, d), lambda i: (0, 0)),
            pl.BlockSpec((d, n_out), lambda i: (0, 0)),
            pl.BlockSpec((1, LANES), lambda i: (0, 0)),
            pl.BlockSpec((1, LANES), lambda i: (0, 0)),
            pl.BlockSpec((tm, LANES), lambda i: (i % n_s, 0)),
            pl.BlockSpec((tm, LANES), lambda i: (i % n_s, 0)),
        ],
        out_specs=pl.BlockSpec((tm, n_out), lambda i: (i, 0)),
        out_shape=jax.ShapeDtypeStruct((t, n_out), BF16),
        compiler_params=pltpu.CompilerParams(
            dimension_semantics=("arbitrary",), vmem_limit_bytes=VMEM_LIMIT),
        name="inproj",
    )(x2, g, w_bf, gq_t, gk_t, cos_t, sin_t)


def _masked_keys(k_ref, km_ref, groups):
    seq = k_ref.shape[0]
    lane = lax.broadcasted_iota(jnp.int32, (seq, LANES), 1)
    for g in range(groups):
        k_g = k_ref[:, g * LANES:(g + 1) * LANES]
        zero = jnp.zeros_like(k_g)
        km_ref[2 * g] = jnp.where(lane < HEAD_DIM, k_g, zero)
        km_ref[2 * g + 1] = jnp.where(lane < HEAD_DIM, zero, k_g)


def _sb_kernel(q_ref, k_ref, v_ref, o_ref, km_ref, *, tq, groups):
    seq = q_ref.shape[0]
    nq = seq // tq
    _masked_keys(k_ref, km_ref, groups)

    row = lax.broadcasted_iota(jnp.int32, (tq, tq), 0)
    col = lax.broadcasted_iota(jnp.int32, (tq, tq), 1)
    past = col < row
    neg_suffix = jnp.where(row >= col, -1.0, 0.0).astype(BF16)
    lane_o = lax.broadcasted_iota(jnp.int32, (tq, LANES), 1)
    gsl = [slice(g * LANES, (g + 1) * LANES) for g in range(groups)]

    def q_block(i, _):
        r0 = pl.multiple_of(i * tq, tq)
        qs = [q_ref[pl.ds(r0, tq), gsl[g]] for g in range(groups)]
        heads = range(2 * groups)

        def tiles(c0, carries, diag):
            z = [_dot_nt(qs[h // 2], km_ref[h, pl.ds(c0, tq), :]) for h in heads]
            zc, n, nb = [], [], []
            for h in heads:
                n_h = jnp.maximum(z[h], 0.0) + jnp.log2(1.0 + jnp.exp2(-jnp.abs(z[h])))
                if diag:
                    n_h = jnp.where(past, n_h, 0.0)
                n.append(n_h)
                nb.append(n_h.astype(BF16))
                zc.append(z[h] + carries[h][0])
            cum = [_dot(nb[h], neg_suffix) for h in heads]
            w = []
            for h in heads:
                w_h = jnp.exp2(zc[h] + cum[h])
                if diag:
                    w_h = jnp.where(past, w_h, 0.0)
                w.append(w_h.astype(BF16))
            return tuple(
                (carries[h][0] - jnp.sum(n[h], axis=-1, keepdims=True),
                 carries[h][1] + _dot(w[h], v_ref[pl.ds(c0, tq), gsl[h // 2]]))
                for h in heads)

        init = tuple((jnp.zeros((tq, 1), F32), jnp.zeros((tq, LANES), F32)) for _ in range(2 * groups))
        carries = tiles(r0, init, True)

        def kv_step(jj, cs):
            return tiles(pl.multiple_of((i - 1 - jj) * tq, tq), cs, False)

        carries = lax.fori_loop(0, i, kv_step, carries)
        for g in range(groups):
            o_ref[pl.ds(r0, tq), gsl[g]] = jnp.where(
                lane_o < HEAD_DIM, carries[2 * g][1], carries[2 * g + 1][1]).astype(o_ref.dtype)
        return 0

    lax.fori_loop(0, nq, q_block, 0)


def _sb_attention(proj, *, n_pairs, q_blk, k_blk, v_blk, tq, groups):
    b, seq, _ = proj.shape
    width = groups * LANES
    steps = n_pairs // groups
    kern = functools.partial(_sb_kernel, tq=tq, groups=groups)
    spec = lambda off: pl.BlockSpec((None, seq, width), lambda bi, hp: (bi, 0, off // groups + hp))
    return pl.pallas_call(
        kern,
        grid=(b, steps),
        in_specs=[spec(q_blk), spec(k_blk), spec(v_blk)],
        out_specs=pl.BlockSpec((None, seq, width), lambda bi, hp: (bi, 0, hp)),
        out_shape=jax.ShapeDtypeStruct((b, seq, n_pairs * LANES), BF16),
        scratch_shapes=[pltpu.VMEM((2 * groups, seq, LANES), BF16)],
        compiler_params=pltpu.CompilerParams(
            dimension_semantics=("arbitrary", "arbitrary"), vmem_limit_bytes=VMEM_LIMIT),
        name="sb_attn",
    )(proj, proj, proj)


def _diff_kernel(shift_ref, q_ref, k_ref, v_ref, lq1_ref, lk1_ref, lq2_ref, lk2_ref, gs_ref, o_ref, km_ref,
                 *, tq, groups, lambda_init):
    seq = q_ref.shape[0]
    nq = seq // tq
    _masked_keys(k_ref, km_ref, groups)

    lam = (jnp.exp(jnp.sum(lq1_ref[...] * lk1_ref[...], axis=-1, keepdims=True))
           - jnp.exp(jnp.sum(lq2_ref[...] * lk2_ref[...], axis=-1, keepdims=True))
           + lambda_init)

    row = lax.broadcasted_iota(jnp.int32, (tq, tq), 0)
    col = lax.broadcasted_iota(jnp.int32, (tq, tq), 1)
    causal = col <= row
    gsl = [slice(g * LANES, (g + 1) * LANES) for g in range(groups)]
    maps = range(2 * groups)

    def finish(r0, g, num1, den1, num2, den2):
        o = num1 / den1 - lam * (num2 / den2)
        ms = jnp.mean(o * o, axis=-1, keepdims=True)
        o = o * lax.rsqrt(ms + SUBLN_EPS) * gs_ref[...]
        o_ref[pl.ds(r0, tq), gsl[g]] = (o * (1.0 - lambda_init)).astype(o_ref.dtype)

    def q_block_static_shift(i, _):
        shift = shift_ref[0]
        r0 = pl.multiple_of(i * tq, tq)
        qs = [q_ref[pl.ds(r0, tq), gsl[g]] for g in range(groups)]

        def tiles(c0, carries, diag):
            z = [_dot_nt(qs[n // 2], km_ref[n, pl.ds(c0, tq), :]) for n in maps]
            p = []
            for n in maps:
                p_n = jnp.exp2(z[n] - shift)
                p.append(jnp.where(causal, p_n, 0.0) if diag else p_n)
            return tuple(
                (carries[n][0] + jnp.sum(p[n], axis=-1, keepdims=True),
                 carries[n][1] + _dot(p[n].astype(BF16), v_ref[pl.ds(c0, tq), gsl[n // 2]]))
                for n in maps)

        init = tuple((jnp.zeros((tq, 1), F32), jnp.zeros((tq, LANES), F32)) for _ in maps)
        carries = tiles(r0, init, True)
        carries = lax.fori_loop(
            0, i, lambda jj, cs: tiles(pl.multiple_of((i - 1 - jj) * tq, tq), cs, False), carries)
        for g in range(groups):
            (l1, a1), (l2, a2) = carries[2 * g], carries[2 * g + 1]
            finish(r0, g, a1, l1, a2, l2)
        return 0

    def q_block_running_max(i, _):
        r0 = pl.multiple_of(i * tq, tq)
        qs = [q_ref[pl.ds(r0, tq), gsl[g]] for g in range(groups)]

        def tiles(c0, carries, diag):
            z = [_dot_nt(qs[n // 2], km_ref[n, pl.ds(c0, tq), :]) for n in maps]
            m_new, alpha, l_new, p = [], [], [], []
            for n in maps:
                m, l, _ = carries[n]
                z_n = jnp.where(causal, z[n], NEG_BIG) if diag else z[n]
                m_n = jnp.maximum(m, jnp.max(z_n, axis=-1, keepdims=True))
                a_n = jnp.exp2(m - m_n)
                p_n = jnp.exp2(z_n - m_n)
                m_new.append(m_n)
                alpha.append(a_n)
                l_new.append(a_n * l + jnp.sum(p_n, axis=-1, keepdims=True))
                p.append(p_n.astype(BF16))
            return tuple(
                (m_new[n], l_new[n],
                 alpha[n] * carries[n][2] + _dot(p[n], v_ref[pl.ds(c0, tq), gsl[n // 2]]))
                for n in maps)

        init = tuple((jnp.full((tq, 1), NEG_BIG, F32), jnp.zeros((tq, 1), F32),
                      jnp.zeros((tq, LANES), F32)) for _ in maps)
        carries = tiles(r0, init, True)
        carries = lax.fori_loop(
            0, i, lambda jj, cs: tiles(pl.multiple_of((i - 1 - jj) * tq, tq), cs, False), carries)
        for g in range(groups):
            (_, l1, a1), (_, l2, a2) = carries[2 * g], carries[2 * g + 1]
            finish(r0, g, a1, l1, a2, l2)
        return 0

    use_static_shift = shift_ref[0] <= MAX_STATIC_SHIFT

    @pl.when(use_static_shift)
    def _():
        lax.fori_loop(0, nq, q_block_static_shift, 0)

    @pl.when(jnp.logical_not(use_static_shift))
    def _():
        lax.fori_loop(0, nq, q_block_running_max, 0)


def _diff_attention(shift, proj, lq1, lk1, lq2, lk2, gs, *, n_heads, q_blk, k_blk, v_blk, tq, groups,
                    lambda_init):
    b, seq, _ = proj.shape
    width = groups * LANES
    steps = n_heads // groups
    kern = functools.partial(_diff_kernel, tq=tq, groups=groups, lambda_init=lambda_init)
    spec = lambda off: pl.BlockSpec((None, seq, width), lambda bi, h: (bi, 0, off // groups + h))
    small = lambda n: pl.BlockSpec((1, n), lambda bi, h: (0, 0))
    return pl.pallas_call(
        kern,
        grid=(b, steps),
        in_specs=[pl.BlockSpec(memory_space=pltpu.SMEM),
                  spec(q_blk), spec(k_blk), spec(v_blk),
                  small(HEAD_DIM), small(HEAD_DIM), small(HEAD_DIM), small(HEAD_DIM), small(LANES)],
        out_specs=pl.BlockSpec((None, seq, width), lambda bi, h: (bi, 0, h)),
        out_shape=jax.ShapeDtypeStruct((b, seq, n_heads * LANES), BF16),
        scratch_shapes=[pltpu.VMEM((2 * groups, seq, LANES), BF16)],
        compiler_params=pltpu.CompilerParams(
            dimension_semantics=("arbitrary", "arbitrary"), vmem_limit_bytes=VMEM_LIMIT),
        name="diff_attn",
    )(shift, proj, proj, proj, lq1, lk1, lq2, lk2, gs)


def _out_ffn_kernel(x_ref, sb_ref, df_ref, wo_sb_ref, wo_df_ref, g_ref, wg_ref, wu_ref, wd_ref, o_ref):
    x1 = x_ref[...] + _dot(sb_ref[...], wo_sb_ref[...]) + _dot(df_ref[...], wo_df_ref[...])
    ms = jnp.mean(x1 * x1, axis=-1, keepdims=True)
    h2 = (x1 * lax.rsqrt(ms + NORM_EPS) * g_ref[...]).astype(BF16)
    gate = _dot(h2, wg_ref[...])
    up = _dot(h2, wu_ref[...])
    act = (gate * jax.nn.sigmoid(gate) * up).astype(BF16)
    o_ref[...] = x1 + _dot(act, wd_ref[...])


def _out_ffn(x2, sb, df, wo_sb, wo_df, g, wg, wu, wd, *, tm):
    t, d = x2.shape
    d_ff = wg.shape[1]
    resident = lambda shape: pl.BlockSpec(shape, lambda i: (0, 0), pipeline_mode=pl.Buffered(1))
    return pl.pallas_call(
        _out_ffn_kernel,
        grid=(t // tm,),
        in_specs=[
            pl.BlockSpec((tm, d), lambda i: (i, 0)),
            pl.BlockSpec((tm, sb.shape[1]), lambda i: (i, 0)),
            pl.BlockSpec((tm, df.shape[1]), lambda i: (i, 0)),
            resident(wo_sb.shape), resident(wo_df.shape),
            pl.BlockSpec((1, d), lambda i: (0, 0)),
            resident((d, d_ff)), resident((d, d_ff)), resident((d_ff, d)),
        ],
        out_specs=pl.BlockSpec((tm, d), lambda i: (i, 0)),
        out_shape=jax.ShapeDtypeStruct((t, d), F32),
        compiler_params=pltpu.CompilerParams(
            dimension_semantics=("arbitrary",), vmem_limit_bytes=VMEM_LIMIT),
        name="out_ffn",
    )(x2, sb, df, wo_sb, wo_df, g, wg, wu, wd)


def _rope_tables(seq):
    half = HEAD_DIM // 2
    inv_freq = ROPE_THETA ** (-jnp.arange(half, dtype=F32) / half)
    ang = jnp.arange(seq, dtype=jnp.int32).astype(F32)[:, None] * inv_freq[None, :]
    cos, sin = jnp.cos(ang), jnp.sin(ang)
    reps = LANES // HEAD_DIM
    cos_t = jnp.tile(jnp.concatenate([cos, cos], axis=-1), (1, reps))
    sin_t = jnp.tile(jnp.concatenate([-sin, sin], axis=-1), (1, reps))
    return cos_t, sin_t


def kernel(x, attn_norm_g, w_in, diff_q_norm_g, diff_k_norm_g, lambda_q1, lambda_k1, lambda_q2, lambda_k2,
           diff_subln_g, w_o, ffn_norm_g, w_gate, w_up, w_down):
    b, seq, d = x.shape
    depth = w_in.shape[0]
    mix_width = w_o.shape[1]
    sb_width = mix_width // 2
    diff_width = mix_width - sb_width
    assert sb_width % LANES == 0 and diff_width % LANES == 0
    n_pairs = sb_width // LANES
    n_dheads = diff_width // LANES
    cos_t, sin_t = _rope_tables(seq)
    reps = LANES // HEAD_DIM

    x2 = x.reshape(b * seq, d)
    for layer in range(depth):
        lambda_init = 0.8 - 0.6 * math.exp(-0.3 * layer)
        proj = _inproj(
            x2, attn_norm_g[layer][None, :], w_in[layer].astype(BF16),
            jnp.tile(diff_q_norm_g[layer], reps)[None, :], jnp.tile(diff_k_norm_g[layer], reps)[None, :],
            cos_t, sin_t, seq=seq, sb_width=sb_width, diff_width=diff_width, tm=512)
        proj = proj.reshape(b, seq, -1)
        sb = _sb_attention(proj, n_pairs=n_pairs, q_blk=0, k_blk=n_pairs, v_blk=2 * n_pairs, tq=256, groups=4)
        d0 = 3 * n_pairs
        shift = (HEAD_DIM * Q_SCALE * 1.02 * jnp.max(jnp.abs(diff_q_norm_g[layer]))
                 * jnp.max(jnp.abs(diff_k_norm_g[layer]))).reshape(1).astype(F32)
        df = _diff_attention(
            shift, proj, lambda_q1[layer][None, :], lambda_k1[layer][None, :], lambda_q2[layer][None, :],
            lambda_k2[layer][None, :], diff_subln_g[layer][None, :],
            n_heads=n_dheads, q_blk=d0, k_blk=d0 + n_dheads, v_blk=d0 + 2 * n_dheads, tq=256,
            groups=4, lambda_init=lambda_init)
        wo = w_o[layer].astype(BF16)
        x2 = _out_ffn(
            x2, sb.reshape(b * seq, sb_width), df.reshape(b * seq, diff_width),
            wo[:sb_width], wo[sb_width:], ffn_norm_g[layer][None, :],
            w_gate[layer].astype(BF16), w_up[layer].astype(BF16), w_down[layer].astype(BF16),
            tm=512)
    return x2.reshape(b, seq, d)
```

```python
import functools
import math

import jax
import jax.numpy as jnp
from jax import lax
from jax.experimental import pallas as pl
from jax.experimental.pallas import tpu as pltpu

HEAD_DIM = 64
LANES = 128
ROPE_THETA = 10000.0
NORM_EPS = 1e-6
SUBLN_EPS = 1e-5
LOG2E = math.log2(math.e)
Q_SCALE = HEAD_DIM ** -0.5 * LOG2E
NEG_BIG = -1e30
MAX_STATIC_SHIFT = 50.0
VMEM_LIMIT = 56 * 1024 * 1024

F32 = jnp.float32
BF16 = jnp.bfloat16


def _dot(a, b):
    return jnp.dot(a, b, preferred_element_type=F32)


def _dot_nt(a, b):
    return lax.dot_general(a, b, (((1,), (1,)), ((), ())), preferred_element_type=F32)


def _inproj_kernel(x_ref, g_ref, w_ref, gq_ref, gk_ref, cos_ref, sin_ref, o_ref, *, sb_width, diff_width):
    x = x_ref[...]
    ms = jnp.mean(x * x, axis=-1, keepdims=True)
    h = (x * lax.rsqrt(ms + NORM_EPS) * g_ref[...]).astype(BF16)

    tm = x.shape[0]
    lane = lax.broadcasted_iota(jnp.int32, (tm, LANES), 1)
    low_head = lane < HEAD_DIM
    first_half = (lane & (HEAD_DIM - 1)) < (HEAD_DIM // 2)
    cos = cos_ref[...]
    sin = sin_ref[...]

    def norm_rope(p, g, scale):
        sq = p * p
        s_lo = jnp.sum(jnp.where(low_head, sq, 0.0), axis=-1, keepdims=True)
        s_hi = jnp.sum(jnp.where(low_head, 0.0, sq), axis=-1, keepdims=True)
        ms2 = jnp.where(low_head, s_lo, s_hi) * (1.0 / HEAD_DIM)
        pn = p * lax.rsqrt(ms2 + NORM_EPS) * g
        swapped = jnp.where(first_half, pltpu.roll(pn, LANES - HEAD_DIM // 2, 1),
                            pltpu.roll(pn, HEAD_DIM // 2, 1))
        out = pn * cos + swapped * sin
        if scale != 1.0:
            out = out * scale
        return out.astype(BF16)

    q_lo, k_lo, v_lo = 0, sb_width, 2 * sb_width
    dq_lo = 3 * sb_width
    dk_lo = dq_lo + diff_width
    dv_lo = dk_lo + diff_width
    n_out = dv_lo + diff_width
    chunk = 512
    starts = sorted(range(0, n_out, chunk), key=lambda c: not dq_lo <= c < dv_lo)
    for c0 in starts:
        p = _dot(h, w_ref[:, c0:c0 + chunk])
        if c0 < k_lo:
            o_ref[:, c0:c0 + chunk] = (p * Q_SCALE).astype(BF16)
        elif dq_lo <= c0 < dv_lo:
            is_q = c0 < dk_lo
            g = gq_ref[...] if is_q else gk_ref[...]
            for l0 in range(0, chunk, LANES):
                o_ref[:, c0 + l0:c0 + l0 + LANES] = norm_rope(
                    p[:, l0:l0 + LANES], g, Q_SCALE if is_q else 1.0)
        else:
            o_ref[:, c0:c0 + chunk] = p.astype(BF16)


def _inproj(x2, g, w_bf, gq_t, gk_t, cos_t, sin_t, *, seq, sb_width, diff_width, tm):
    t, d = x2.shape
    n_out = w_bf.shape[1]
    n_s = seq // tm
    kern = functools.partial(_inproj_kernel, sb_width=sb_width, diff_width=diff_width)
    return pl.pallas_call(
        kern,
        grid=(t // tm,),
        in_specs=[
            pl.BlockSpec((tm, d), lambda i: (i, 0)),
            pl.BlockSpec((1, d), lambda i: (0, 0)),
            pl.BlockSpec((d, n_out), lambda i: (0, 0)),
            pl.BlockSpec((1, LANES), lambda i: (0, 0)),
            pl.BlockSpec((1, LANES), lambda i: (0, 0)),
            pl.BlockSpec((tm, LANES), lambda i: (i % n_s, 0)),
            pl.BlockSpec((tm, LANES), lambda i: (i % n_s, 0)),
        ],
        out_specs=pl.BlockSpec((tm, n_out), lambda i: (i, 0)),
        out_shape=jax.ShapeDtypeStruct((t, n_out), BF16),
        compiler_params=pltpu.CompilerParams(
            dimension_semantics=("arbitrary",), vmem_limit_bytes=VMEM_LIMIT),
        name="inproj",
    )(x2, g, w_bf, gq_t, gk_t, cos_t, sin_t)


def _masked_keys(k_ref, km_ref, groups):
    seq = k_ref.shape[0]
    lane = lax.broadcasted_iota(jnp.int32, (seq, LANES), 1)
    for g in range(groups):
        k_g = k_ref[:, g * LANES:(g + 1) * LANES]
        zero = jnp.zeros_like(k_g)
        km_ref[2 * g] = jnp.where(lane < HEAD_DIM, k_g, zero)
        km_ref[2 * g + 1] = jnp.where(lane < HEAD_DIM, zero, k_g)


def _sb_kernel(q_ref, k_ref, v_ref, o_ref, km_ref, rs_ref, acc_ref, *, tq, groups):
    seq = q_ref.shape[0]
    nq = seq // tq
    _masked_keys(k_ref, km_ref, groups)

    row = lax.broadcasted_iota(jnp.int32, (tq, tq), 0)
    col = lax.broadcasted_iota(jnp.int32, (tq, tq), 1)
    past = col < row
    neg_suffix = jnp.where(row >= col, -1.0, 0.0).astype(BF16)
    lane_o = lax.broadcasted_iota(jnp.int32, (tq, LANES), 1)
    gsl = [slice(g * LANES, (g + 1) * LANES) for g in range(groups)]
    heads = range(2 * groups)

    def q_block(i, _):
        r0 = pl.multiple_of(i * tq, tq)
        qs = [q_ref[pl.ds(r0, tq), gsl[g]] for g in range(groups)]

        def tiles(cols, diag):
            chains = [(h, c0) for h in heads for c0 in cols]
            z = [_dot_nt(qs[h // 2], km_ref[h, pl.ds(c0, tq), :]) for h, c0 in chains]
            rowsum = [jnp.zeros((tq, 1), F32) if diag else rs_ref[h] for h in heads]
            zc, nb = [], []
            for idx, (h, _) in enumerate(chains):
                n_c = jnp.maximum(z[idx], 0.0) + jnp.log2(1.0 + jnp.exp2(-jnp.abs(z[idx])))
                if diag:
                    n_c = jnp.where(past, n_c, 0.0)
                nb.append(n_c.astype(BF16))
                zc.append(z[idx] + rowsum[h])
                rowsum[h] = rowsum[h] - jnp.sum(n_c, axis=-1, keepdims=True)
            cum = [_dot(nb[idx], neg_suffix) for idx in range(len(chains))]
            w = []
            for idx in range(len(chains)):
                w_c = jnp.exp2(zc[idx] + cum[idx])
                if diag:
                    w_c = jnp.where(past, w_c, 0.0)
                w.append(w_c.astype(BF16))
            acc = [None if diag else acc_ref[h] for h in heads]
            for idx, (h, c0) in enumerate(chains):
                pv = _dot(w[idx], v_ref[pl.ds(c0, tq), gsl[h // 2]])
                acc[h] = pv if acc[h] is None else acc[h] + pv
            for h in heads:
                rs_ref[h] = rowsum[h]
                acc_ref[h] = acc[h]

        tiles([r0], True)

        @pl.when(i % 2 == 1)
        def _():
            tiles([pl.multiple_of(r0 - tq, tq)], False)

        def two_tiles(jj, _):
            right = pl.multiple_of(r0 - (i % 2) * tq - (2 * jj + 1) * tq, tq)
            tiles([right, pl.multiple_of(right - tq, tq)], False)
            return 0

        lax.fori_loop(0, i // 2, two_tiles, 0)
        for g in range(groups):
            o_ref[pl.ds(r0, tq), gsl[g]] = jnp.where(
                lane_o < HEAD_DIM, acc_ref[2 * g], acc_ref[2 * g + 1]).astype(o_ref.dtype)
        return 0

    lax.fori_loop(0, nq, q_block, 0)


def _sb_attention(proj, *, n_pairs, q_blk, k_blk, v_blk, tq, groups):
    b, seq, _ = proj.shape
    width = groups * LANES
    steps = n_pairs // groups
    kern = functools.partial(_sb_kernel, tq=tq, groups=groups)
    spec = lambda off: pl.BlockSpec((None, seq, width), lambda bi, hp: (bi, 0, off // groups + hp))
    return pl.pallas_call(
        kern,
        grid=(b, steps),
        in_specs=[spec(q_blk), spec(k_blk), spec(v_blk)],
        out_specs=pl.BlockSpec((None, seq, width), lambda bi, hp: (bi, 0, hp)),
        out_shape=jax.ShapeDtypeStruct((b, seq, n_pairs * LANES), BF16),
        scratch_shapes=[pltpu.VMEM((2 * groups, seq, LANES), BF16),
                        pltpu.VMEM((2 * groups, tq, 1), F32),
                        pltpu.VMEM((2 * groups, tq, LANES), F32)],
        compiler_params=pltpu.CompilerParams(
            dimension_semantics=("arbitrary", "arbitrary"), vmem_limit_bytes=VMEM_LIMIT),
        name="sb_attn",
    )(proj, proj, proj)


def _diff_kernel(shift_ref, q_ref, k_ref, v_ref, lq1_ref, lk1_ref, lq2_ref, lk2_ref, gs_ref, o_ref, km_ref,
                 den_ref, acc_ref, *, tq, groups, lambda_init):
    seq = q_ref.shape[0]
    nq = seq // tq
    _masked_keys(k_ref, km_ref, groups)

    lam = (jnp.exp(jnp.sum(lq1_ref[...] * lk1_ref[...], axis=-1, keepdims=True))
           - jnp.exp(jnp.sum(lq2_ref[...] * lk2_ref[...], axis=-1, keepdims=True))
           + lambda_init)

    row = lax.broadcasted_iota(jnp.int32, (tq, tq), 0)
    col = lax.broadcasted_iota(jnp.int32, (tq, tq), 1)
    causal = col <= row
    gsl = [slice(g * LANES, (g + 1) * LANES) for g in range(groups)]
    maps = range(2 * groups)

    def finish(r0, g, num1, den1, num2, den2):
        o = num1 / den1 - lam * (num2 / den2)
        ms = jnp.mean(o * o, axis=-1, keepdims=True)
        o = o * lax.rsqrt(ms + SUBLN_EPS) * gs_ref[...]
        o_ref[pl.ds(r0, tq), gsl[g]] = (o * (1.0 - lambda_init)).astype(o_ref.dtype)

    def q_block_static_shift(i, _):
        shift = shift_ref[0]
        r0 = pl.multiple_of(i * tq, tq)
        qs = [q_ref[pl.ds(r0, tq), gsl[g]] for g in range(groups)]

        def tiles(cols, diag):
            chains = [(n, c0) for n in maps for c0 in cols]
            z = [_dot_nt(qs[n // 2], km_ref[n, pl.ds(c0, tq), :]) for n, c0 in chains]
            den = [None if diag else den_ref[n] for n in maps]
            acc = [None if diag else acc_ref[n] for n in maps]
            p = []
            for idx, (n, _) in enumerate(chains):
                p_c = jnp.exp2(z[idx] - shift)
                if diag:
                    p_c = jnp.where(causal, p_c, 0.0)
                p_sum = jnp.sum(p_c, axis=-1, keepdims=True)
                den[n] = p_sum if den[n] is None else den[n] + p_sum
                p.append(p_c.astype(BF16))
            for idx, (n, c0) in enumerate(chains):
                pv = _dot(p[idx], v_ref[pl.ds(c0, tq), gsl[n // 2]])
                acc[n] = pv if acc[n] is None else acc[n] + pv
            for n in maps:
                den_ref[n] = den[n]
                acc_ref[n] = acc[n]

        tiles([r0], True)

        @pl.when(i % 2 == 1)
        def _():
            tiles([pl.multiple_of(r0 - tq, tq)], False)

        def two_tiles(jj, _):
            right = pl.multiple_of(r0 - (i % 2) * tq - (2 * jj + 1) * tq, tq)
            tiles([right, pl.multiple_of(right - tq, tq)], False)
            return 0

        lax.fori_loop(0, i // 2, two_tiles, 0)
        for g in range(groups):
            finish(r0, g, acc_ref[2 * g], den_ref[2 * g], acc_ref[2 * g + 1], den_ref[2 * g + 1])
        return 0

    def q_block_running_max(i, _):
        r0 = pl.multiple_of(i * tq, tq)
        qs = [q_ref[pl.ds(r0, tq), gsl[g]] for g in range(groups)]

        def tiles(c0, carries, diag):
            z = [_dot_nt(qs[n // 2], km_ref[n, pl.ds(c0, tq), :]) for n in maps]
            m_new, alpha, l_new, p = [], [], [], []
            for n in maps:
                m, l, _ = carries[n]
                z_n = jnp.where(causal, z[n], NEG_BIG) if diag else z[n]
                m_n = jnp.maximum(m, jnp.max(z_n, axis=-1, keepdims=True))
                a_n = jnp.exp2(m - m_n)
                p_n = jnp.exp2(z_n - m_n)
                m_new.append(m_n)
                alpha.append(a_n)
                l_new.append(a_n * l + jnp.sum(p_n, axis=-1, keepdims=True))
                p.append(p_n.astype(BF16))
            return tuple(
                (m_new[n], l_new[n],
                 alpha[n] * carries[n][2] + _dot(p[n], v_ref[pl.ds(c0, tq), gsl[n // 2]]))
                for n in maps)

        init = tuple((jnp.full((tq, 1), NEG_BIG, F32), jnp.zeros((tq, 1), F32),
                      jnp.zeros((tq, LANES), F32)) for _ in maps)
        carries = tiles(r0, init, True)
        carries = lax.fori_loop(
            0, i, lambda jj, cs: tiles(pl.multiple_of((i - 1 - jj) * tq, tq), cs, False), carries)
        for g in range(groups):
            (_, l1, a1), (_, l2, a2) = carries[2 * g], carries[2 * g + 1]
            finish(r0, g, a1, l1, a2, l2)
        return 0

    use_static_shift = shift_ref[0] <= MAX_STATIC_SHIFT

    @pl.when(use_static_shift)
    def _():
        lax.fori_loop(0, nq, q_block_static_shift, 0)

    @pl.when(jnp.logical_not(use_static_shift))
    def _():
        lax.fori_loop(0, nq, q_block_running_max, 0)


def _diff_attention(shift, proj, lq1, lk1, lq2, lk2, gs, *, n_heads, q_blk, k_blk, v_blk, tq, groups,
                    lambda_init):
    b, seq, _ = proj.shape
    width = groups * LANES
    steps = n_heads // groups
    kern = functools.partial(_diff_kernel, tq=tq, groups=groups, lambda_init=lambda_init)
    spec = lambda off: pl.BlockSpec((None, seq, width), lambda bi, h: (bi, 0, off // groups + h))
    small = lambda n: pl.BlockSpec((1, n), lambda bi, h: (0, 0))
    return pl.pallas_call(
        kern,
        grid=(b, steps),
        in_specs=[pl.BlockSpec(memory_space=pltpu.SMEM),
                  spec(q_blk), spec(k_blk), spec(v_blk),
                  small(HEAD_DIM), small(HEAD_DIM), small(HEAD_DIM), small(HEAD_DIM), small(LANES)],
        out_specs=pl.BlockSpec((None, seq, width), lambda bi, h: (bi, 0, h)),
        out_shape=jax.ShapeDtypeStruct((b, seq, n_heads * LANES), BF16),
        scratch_shapes=[pltpu.VMEM((2 * groups, seq, LANES), BF16),
                        pltpu.VMEM((2 * groups, tq, 1), F32),
                        pltpu.VMEM((2 * groups, tq, LANES), F32)],
        compiler_params=pltpu.CompilerParams(
            dimension_semantics=("arbitrary", "arbitrary"), vmem_limit_bytes=VMEM_LIMIT),
        name="diff_attn",
    )(shift, proj, proj, proj, lq1, lk1, lq2, lk2, gs)


def _out_ffn_kernel(x_ref, sb_ref, df_ref, wo_sb_ref, wo_df_ref, g_ref, wg_ref, wu_ref, wd_ref, o_ref):
    x1 = x_ref[...] + _dot(sb_ref[...], wo_sb_ref[...]) + _dot(df_ref[...], wo_df_ref[...])
    ms = jnp.mean(x1 * x1, axis=-1, keepdims=True)
    h2 = (x1 * lax.rsqrt(ms + NORM_EPS) * g_ref[...]).astype(BF16)
    gate = _dot(h2, wg_ref[...])
    up = _dot(h2, wu_ref[...])
    act = (gate * jax.nn.sigmoid(gate) * up).astype(BF16)
    o_ref[...] = x1 + _dot(act, wd_ref[...])


def _out_ffn(x2, sb, df, wo_sb, wo_df, g, wg, wu, wd, *, tm):
    t, d = x2.shape
    d_ff = wg.shape[1]
    resident = lambda shape: pl.BlockSpec(shape, lambda i: (0, 0), pipeline_mode=pl.Buffered(1))
    return pl.pallas_call(
        _out_ffn_kernel,
        grid=(t // tm,),
        in_specs=[
            pl.BlockSpec((tm, d), lambda i: (i, 0)),
            pl.BlockSpec((tm, sb.shape[1]), lambda i: (i, 0)),
            pl.BlockSpec((tm, df.shape[1]), lambda i: (i, 0)),
            resident(wo_sb.shape), resident(wo_df.shape),
            pl.BlockSpec((1, d), lambda i: (0, 0)),
            resident((d, d_ff)), resident((d, d_ff)), resident((d_ff, d)),
        ],
        out_specs=pl.BlockSpec((tm, d), lambda i: (i, 0)),
        out_shape=jax.ShapeDtypeStruct((t, d), F32),
        compiler_params=pltpu.CompilerParams(
            dimension_semantics=("arbitrary",), vmem_limit_bytes=VMEM_LIMIT),
        name="out_ffn",
    )(x2, sb, df, wo_sb, wo_df, g, wg, wu, wd)


def _rope_tables(seq):
    half = HEAD_DIM // 2
    inv_freq = ROPE_THETA ** (-jnp.arange(half, dtype=F32) / half)
    ang = jnp.arange(seq, dtype=jnp.int32).astype(F32)[:, None] * inv_freq[None, :]
    cos, sin = jnp.cos(ang), jnp.sin(ang)
    reps = LANES // HEAD_DIM
    cos_t = jnp.tile(jnp.concatenate([cos, cos], axis=-1), (1, reps))
    sin_t = jnp.tile(jnp.concatenate([-sin, sin], axis=-1), (1, reps))
    return cos_t, sin_t


def kernel(x, attn_norm_g, w_in, diff_q_norm_g, diff_k_norm_g, lambda_q1, lambda_k1, lambda_q2, lambda_k2,
           diff_subln_g, w_o, ffn_norm_g, w_gate, w_up, w_down):
    b, seq, d = x.shape
    depth = w_in.shape[0]
    mix_width = w_o.shape[1]
    sb_width = mix_width // 2
    diff_width = mix_width - sb_width
    assert sb_width % LANES == 0 and diff_width % LANES == 0
    n_pairs = sb_width // LANES
    n_dheads = diff_width // LANES
    cos_t, sin_t = _rope_tables(seq)
    reps = LANES // HEAD_DIM

    x2 = x.reshape(b * seq, d)
    for layer in range(depth):
        lambda_init = 0.8 - 0.6 * math.exp(-0.3 * layer)
        proj = _inproj(
            x2, attn_norm_g[layer][None, :], w_in[layer].astype(BF16),
            jnp.tile(diff_q_norm_g[layer], reps)[None, :], jnp.tile(diff_k_norm_g[layer], reps)[None, :],
            cos_t, sin_t, seq=seq, sb_width=sb_width, diff_width=diff_width, tm=512)
        proj = proj.reshape(b, seq, -1)
        sb = _sb_attention(proj, n_pairs=n_pairs, q_blk=0, k_blk=n_pairs, v_blk=2 * n_pairs, tq=256, groups=4)
        d0 = 3 * n_pairs
        shift = (HEAD_DIM * Q_SCALE * 1.02 * jnp.max(jnp.abs(diff_q_norm_g[layer]))
                 * jnp.max(jnp.abs(diff_k_norm_g[layer]))).reshape(1).astype(F32)
        df = _diff_attention(
            shift, proj, lambda_q1[layer][None, :], lambda_k1[layer][None, :], lambda_q2[layer][None, :],
            lambda_k2[layer][None, :], diff_subln_g[layer][None, :],
            n_heads=n_dheads, q_blk=d0, k_blk=d0 + n_dheads, v_blk=d0 + 2 * n_dheads, tq=256,
            groups=4, lambda_init=lambda_init)
        wo = w_o[layer].astype(BF16)
        x2 = _out_ffn(
            x2, sb.reshape(b * seq, sb_width), df.reshape(b * seq, diff_width),
            wo[:sb_width], wo[sb_width:], ffn_norm_g[layer][None, :],
            w_gate[layer].astype(BF16), w_up[layer].astype(BF16), w_down[layer].astype(BF16),
            tm=512)
    return x2.reshape(b, seq, d)
```

```python
import functools
import math

import jax
import jax.numpy as jnp
from jax import lax
from jax.experimental import pallas as pl
from jax.experimental.pallas import tpu as pltpu

HEAD_DIM = 64
LANES = 128
ROPE_THETA = 10000.0
NORM_EPS = 1e-6
SUBLN_EPS = 1e-5
LOG2E = math.log2(math.e)
Q_SCALE = HEAD_DIM ** -0.5 * LOG2E
NEG_BIG = -1e30
MAX_STATIC_SHIFT = 50.0
VMEM_LIMIT = 56 * 1024 * 1024

F32 = jnp.float32
BF16 = jnp.bfloat16


def _dot(a, b):
    return jnp.dot(a, b, preferred_element_type=F32)


def _dot_nt(a, b):
    return lax.dot_general(a, b, (((1,), (1,)), ((), ())), preferred_element_type=F32)


def _inproj_kernel(x_ref, g_ref, w_ref, gq_ref, gk_ref, cos_ref, sin_ref, o_ref, *, sb_width, diff_width):
    x = x_ref[...]
    ms = jnp.mean(x * x, axis=-1, keepdims=True)
    h = (x * lax.rsqrt(ms + NORM_EPS) * g_ref[...]).astype(BF16)

    tm = x.shape[0]
    lane = lax.broadcasted_iota(jnp.int32, (tm, LANES), 1)
    low_head = lane < HEAD_DIM
    first_half = (lane & (HEAD_DIM - 1)) < (HEAD_DIM // 2)
    cos = cos_ref[...]
    sin = sin_ref[...]

    def norm_rope(p, g, scale):
        sq = p * p
        s_lo = jnp.sum(jnp.where(low_head, sq, 0.0), axis=-1, keepdims=True)
        s_hi = jnp.sum(jnp.where(low_head, 0.0, sq), axis=-1, keepdims=True)
        ms2 = jnp.where(low_head, s_lo, s_hi) * (1.0 / HEAD_DIM)
        pn = p * lax.rsqrt(ms2 + NORM_EPS) * g
        swapped = jnp.where(first_half, pltpu.roll(pn, LANES - HEAD_DIM // 2, 1),
                            pltpu.roll(pn, HEAD_DIM // 2, 1))
        out = pn * cos + swapped * sin
        if scale != 1.0:
            out = out * scale
        return out.astype(BF16)

    q_lo, k_lo, v_lo = 0, sb_width, 2 * sb_width
    dq_lo = 3 * sb_width
    dk_lo = dq_lo + diff_width
    dv_lo = dk_lo + diff_width
    n_out = dv_lo + diff_width
    chunk = 512
    starts = sorted(range(0, n_out, chunk), key=lambda c: not dq_lo <= c < dv_lo)
    for c0 in starts:
        p = _dot(h, w_ref[:, c0:c0 + chunk])
        if c0 < k_lo:
            o_ref[:, c0:c0 + chunk] = (p * Q_SCALE).astype(BF16)
        elif dq_lo <= c0 < dv_lo:
            is_q = c0 < dk_lo
            g = gq_ref[...] if is_q else gk_ref[...]
            for l0 in range(0, chunk, LANES):
                o_ref[:, c0 + l0:c0 + l0 + LANES] = norm_rope(
                    p[:, l0:l0 + LANES], g, Q_SCALE if is_q else 1.0)
        else:
            o_ref[:, c0:c0 + chunk] = p.astype(BF16)


def _inproj(x2, g, w_bf, gq_t, gk_t, cos_t, sin_t, *, seq, sb_width, diff_width, tm):
    t, d = x2.shape
    n_out = w_bf.shape[1]
    n_s = seq // tm
    kern = functools.partial(_inproj_kernel, sb_width=sb_width, diff_width=diff_width)
    return pl.pallas_call(
        kern,
        grid=(t // tm,),
        in_specs=[
            pl.BlockSpec((tm, d), lambda i: (i, 0)),
            pl.BlockSpec((1, d), lambda i: (0, 0)),
            pl.BlockSpec((d, n_out), lambda i: (0, 0)),
            pl.BlockSpec((1, LANES), lambda i: (0, 0)),
            pl.BlockSpec((1, LANES), lambda i: (0, 0)),
            pl.BlockSpec((tm, LANES), lambda i: (i % n_s, 0)),
            pl.BlockSpec((tm, LANES), lambda i: (i % n_s, 0)),
        ],
        out_specs=pl.BlockSpec((tm, n_out), lambda i: (i, 0)),
        out_shape=jax.ShapeDtypeStruct((t, n_out), BF16),
        compiler_params=pltpu.CompilerParams(
            dimension_semantics=("arbitrary",), vmem_limit_bytes=VMEM_LIMIT),
        name="inproj",
    )(x2, g, w_bf, gq_t, gk_t, cos_t, sin_t)


def _masked_keys(k_ref, km_ref, groups):
    seq = k_ref.shape[0]
    lane = lax.broadcasted_iota(jnp.int32, (seq, LANES), 1)
    for g in range(groups):
        k_g = k_ref[:, g * LANES:(g + 1) * LANES]
        zero = jnp.zeros_like(k_g)
        km_ref[2 * g] = jnp.where(lane < HEAD_DIM, k_g, zero)
        km_ref[2 * g + 1] = jnp.where(lane < HEAD_DIM, zero, k_g)


def _sb_kernel(q_ref, k_ref, v_ref, o_ref, km_ref, rs_ref, acc_ref, *, tq, groups):
    seq = q_ref.shape[0]
    nq = seq // tq
    _masked_keys(k_ref, km_ref, groups)

    row = lax.broadcasted_iota(jnp.int32, (tq, tq), 0)
    col = lax.broadcasted_iota(jnp.int32, (tq, tq), 1)
    past = col < row
    neg_suffix = jnp.where(row >= col, -1.0, 0.0).astype(BF16)
    lane_o = lax.broadcasted_iota(jnp.int32, (tq, LANES), 1)
    gsl = [slice(g * LANES, (g + 1) * LANES) for g in range(groups)]
    heads = range(2 * groups)

    def q_block(i, _):
        r0 = pl.multiple_of(i * tq, tq)
        qs = [q_ref[pl.ds(r0, tq), gsl[g]] for g in range(groups)]

        def tiles(cols, diag):
            chains = [(h, c0) for h in heads for c0 in cols]
            z = [_dot_nt(qs[h // 2], km_ref[h, pl.ds(c0, tq), :]) for h, c0 in chains]
            rowsum = [jnp.zeros((tq, 1), F32) if diag else rs_ref[h] for h in heads]
            zc, nb = [], []
            for idx, (h, _) in enumerate(chains):
                n_c = jnp.maximum(z[idx], 0.0) + jnp.log2(1.0 + jnp.exp2(-jnp.abs(z[idx])))
                if diag:
                    n_c = jnp.where(past, n_c, 0.0)
                nb.append(n_c.astype(BF16))
                zc.append(z[idx] + rowsum[h])
                rowsum[h] = rowsum[h] - jnp.sum(n_c, axis=-1, keepdims=True)
            cum = [_dot(nb[idx], neg_suffix) for idx in range(len(chains))]
            w = []
            for idx in range(len(chains)):
                w_c = jnp.exp2(zc[idx] + cum[idx])
                if diag:
                    w_c = jnp.where(past, w_c, 0.0)
                w.append(w_c.astype(BF16))
            acc = [None if diag else acc_ref[h] for h in heads]
            for idx, (h, c0) in enumerate(chains):
                pv = _dot(w[idx], v_ref[pl.ds(c0, tq), gsl[h // 2]])
                acc[h] = pv if acc[h] is None else acc[h] + pv
            for h in heads:
                rs_ref[h] = rowsum[h]
                acc_ref[h] = acc[h]

        tiles([r0], True)

        @pl.when(i % 2 == 1)
        def _():
            tiles([pl.multiple_of(r0 - tq, tq)], False)

        def two_tiles(jj, _):
            right = pl.multiple_of(r0 - (i % 2) * tq - (2 * jj + 1) * tq, tq)
            tiles([right, pl.multiple_of(right - tq, tq)], False)
            return 0

        lax.fori_loop(0, i // 2, two_tiles, 0)
        for g in range(groups):
            o_ref[pl.ds(r0, tq), gsl[g]] = jnp.where(
                lane_o < HEAD_DIM, acc_ref[2 * g], acc_ref[2 * g + 1]).astype(o_ref.dtype)
        return 0

    lax.fori_loop(0, nq, q_block, 0)


def _sb_attention(proj, *, n_pairs, q_blk, k_blk, v_blk, tq, groups):
    b, seq, _ = proj.shape
    width = groups * LANES
    steps = n_pairs // groups
    kern = functools.partial(_sb_kernel, tq=tq, groups=groups)
    spec = lambda off: pl.BlockSpec((None, seq, width), lambda bi, hp: (bi, 0, off // groups + hp))
    return pl.pallas_call(
        kern,
        grid=(b, steps),
        in_specs=[spec(q_blk), spec(k_blk), spec(v_blk)],
        out_specs=pl.BlockSpec((None, seq, width), lambda bi, hp: (bi, 0, hp)),
        out_shape=jax.ShapeDtypeStruct((b, seq, n_pairs * LANES), BF16),
        scratch_shapes=[pltpu.VMEM((2 * groups, seq, LANES), BF16),
                        pltpu.VMEM((2 * groups, tq, 1), F32),
                        pltpu.VMEM((2 * groups, tq, LANES), F32)],
        compiler_params=pltpu.CompilerParams(
            dimension_semantics=("arbitrary", "arbitrary"), vmem_limit_bytes=VMEM_LIMIT),
        name="sb_attn",
    )(proj, proj, proj)


def _diff_kernel(shift_ref, q_ref, k_ref, v_ref, lq1_ref, lk1_ref, lq2_ref, lk2_ref, gs_ref, o_ref, km_ref,
                 den_ref, acc_ref, *, tq, groups, lambda_init):
    seq = q_ref.shape[0]
    nq = seq // tq
    _masked_keys(k_ref, km_ref, groups)

    lam = (jnp.exp(jnp.sum(lq1_ref[...] * lk1_ref[...], axis=-1, keepdims=True))
           - jnp.exp(jnp.sum(lq2_ref[...] * lk2_ref[...], axis=-1, keepdims=True))
           + lambda_init)

    row = lax.broadcasted_iota(jnp.int32, (tq, tq), 0)
    col = lax.broadcasted_iota(jnp.int32, (tq, tq), 1)
    causal = col <= row
    gsl = [slice(g * LANES, (g + 1) * LANES) for g in range(groups)]
    maps = range(2 * groups)

    def finish(r0, g, num1, den1, num2, den2):
        o = num1 / den1 - lam * (num2 / den2)
        ms = jnp.mean(o * o, axis=-1, keepdims=True)
        o = o * lax.rsqrt(ms + SUBLN_EPS) * gs_ref[...]
        o_ref[pl.ds(r0, tq), gsl[g]] = (o * (1.0 - lambda_init)).astype(o_ref.dtype)

    def q_block_static_shift(i, _):
        shift = shift_ref[0]
        r0 = pl.multiple_of(i * tq, tq)
        qs = [q_ref[pl.ds(r0, tq), gsl[g]] for g in range(groups)]

        def tiles(cols, diag):
            chains = [(n, c0) for n in maps for c0 in cols]
            z = [_dot_nt(qs[n // 2], km_ref[n, pl.ds(c0, tq), :]) for n, c0 in chains]
            den = [None if diag else den_ref[n] for n in maps]
            acc = [None if diag else acc_ref[n] for n in maps]
            p = []
            for idx, (n, _) in enumerate(chains):
                p_c = jnp.exp2(z[idx] - shift)
                if diag:
                    p_c = jnp.where(causal, p_c, 0.0)
                p_sum = jnp.sum(p_c, axis=-1, keepdims=True)
                den[n] = p_sum if den[n] is None else den[n] + p_sum
                p.append(p_c.astype(BF16))
            for idx, (n, c0) in enumerate(chains):
                pv = _dot(p[idx], v_ref[pl.ds(c0, tq), gsl[n // 2]])
                acc[n] = pv if acc[n] is None else acc[n] + pv
            for n in maps:
                den_ref[n] = den[n]
                acc_ref[n] = acc[n]

        tiles([r0], True)

        @pl.when(i % 2 == 1)
        def _():
            tiles([pl.multiple_of(r0 - tq, tq)], False)

        def two_tiles(jj, _):
            right = pl.multiple_of(r0 - (i % 2) * tq - (2 * jj + 1) * tq, tq)
            tiles([right, pl.multiple_of(right - tq, tq)], False)
            return 0

        lax.fori_loop(0, i // 2, two_tiles, 0)
        for g in range(groups):
            finish(r0, g, acc_ref[2 * g], den_ref[2 * g], acc_ref[2 * g + 1], den_ref[2 * g + 1])
        return 0

    def q_block_running_max(i, _):
        r0 = pl.multiple_of(i * tq, tq)
        qs = [q_ref[pl.ds(r0, tq), gsl[g]] for g in range(groups)]

        def tiles(c0, carries, diag):
            z = [_dot_nt(qs[n // 2], km_ref[n, pl.ds(c0, tq), :]) for n in maps]
            m_new, alpha, l_new, p = [], [], [], []
            for n in maps:
                m, l, _ = carries[n]
                z_n = jnp.where(causal, z[n], NEG_BIG) if diag else z[n]
                m_n = jnp.maximum(m, jnp.max(z_n, axis=-1, keepdims=True))
                a_n = jnp.exp2(m - m_n)
                p_n = jnp.exp2(z_n - m_n)
                m_new.append(m_n)
                alpha.append(a_n)
                l_new.append(a_n * l + jnp.sum(p_n, axis=-1, keepdims=True))
                p.append(p_n.astype(BF16))
            return tuple(
                (m_new[n], l_new[n],
                 alpha[n] * carries[n][2] + _dot(p[n], v_ref[pl.ds(c0, tq), gsl[n // 2]]))
                for n in maps)

        init = tuple((jnp.full((tq, 1), NEG_BIG, F32), jnp.zeros((tq, 1), F32),
                      jnp.zeros((tq, LANES), F32)) for _ in maps)
        carries = tiles(r0, init, True)
        carries = lax.fori_loop(
            0, i, lambda jj, cs: tiles(pl.multiple_of((i - 1 - jj) * tq, tq), cs, False), carries)
        for g in range(groups):
            (_, l1, a1), (_, l2, a2) = carries[2 * g], carries[2 * g + 1]
            finish(r0, g, a1, l1, a2, l2)
        return 0

    use_static_shift = shift_ref[0] <= MAX_STATIC_SHIFT

    @pl.when(use_static_shift)
    def _():
        lax.fori_loop(0, nq, q_block_static_shift, 0)

    @pl.when(jnp.logical_not(use_static_shift))
    def _():
        lax.fori_loop(0, nq, q_block_running_max, 0)


def _diff_attention(shift, proj, lq1, lk1, lq2, lk2, gs, *, n_heads, q_blk, k_blk, v_blk, tq, groups,
                    lambda_init):
    b, seq, _ = proj.shape
    width = groups * LANES
    steps = n_heads // groups
    kern = functools.partial(_diff_kernel, tq=tq, groups=groups, lambda_init=lambda_init)
    spec = lambda off: pl.BlockSpec((None, seq, width), lambda bi, h: (bi, 0, off // groups + h))
    small = lambda n: pl.BlockSpec((1, n), lambda bi, h: (0, 0))
    return pl.pallas_call(
        kern,
        grid=(b, steps),
        in_specs=[pl.BlockSpec(memory_space=pltpu.SMEM),
                  spec(q_blk), spec(k_blk), spec(v_blk),
                  small(HEAD_DIM), small(HEAD_DIM), small(HEAD_DIM), small(HEAD_DIM), small(LANES)],
        out_specs=pl.BlockSpec((None, seq, width), lambda bi, h: (bi, 0, h)),
        out_shape=jax.ShapeDtypeStruct((b, seq, n_heads * LANES), BF16),
        scratch_shapes=[pltpu.VMEM((2 * groups, seq, LANES), BF16),
                        pltpu.VMEM((2 * groups, tq, 1), F32),
                        pltpu.VMEM((2 * groups, tq, LANES), F32)],
        compiler_params=pltpu.CompilerParams(
            dimension_semantics=("arbitrary", "arbitrary"), vmem_limit_bytes=VMEM_LIMIT),
        name="diff_attn",
    )(shift, proj, proj, proj, lq1, lk1, lq2, lk2, gs)


def _attn_kernel(shift_ref, sq_ref, sk_ref, sv_ref, dq_ref, dk_ref, dv_ref,
                 lq1_ref, lk1_ref, lq2_ref, lk2_ref, gs_ref, so_ref, do_ref,
                 skm_ref, dkm_ref, rs_ref, sacc_ref, den_ref, dacc_ref, *, tq, groups, lambda_init):
    seq = sq_ref.shape[0]
    nq = seq // tq
    _masked_keys(sk_ref, skm_ref, groups)
    _masked_keys(dk_ref, dkm_ref, groups)

    lam = (jnp.exp(jnp.sum(lq1_ref[...] * lk1_ref[...], axis=-1, keepdims=True))
           - jnp.exp(jnp.sum(lq2_ref[...] * lk2_ref[...], axis=-1, keepdims=True))
           + lambda_init)
    shift = shift_ref[0]

    row = lax.broadcasted_iota(jnp.int32, (tq, tq), 0)
    col = lax.broadcasted_iota(jnp.int32, (tq, tq), 1)
    past = col < row
    causal = col <= row
    neg_suffix = jnp.where(row >= col, -1.0, 0.0).astype(BF16)
    lane_o = lax.broadcasted_iota(jnp.int32, (tq, LANES), 1)
    gsl = [slice(g * LANES, (g + 1) * LANES) for g in range(groups)]
    chains = range(2 * groups)

    def q_block(i, _):
        r0 = pl.multiple_of(i * tq, tq)
        sqs = [sq_ref[pl.ds(r0, tq), gsl[g]] for g in range(groups)]
        dqs = [dq_ref[pl.ds(r0, tq), gsl[g]] for g in range(groups)]

        def tiles(cols, diag):
            work = [(c, c0) for c in chains for c0 in cols]
            z = [_dot_nt(sqs[c // 2], skm_ref[c, pl.ds(c0, tq), :]) for c, c0 in work]
            y = [_dot_nt(dqs[c // 2], dkm_ref[c, pl.ds(c0, tq), :]) for c, c0 in work]

            rowsum = [jnp.zeros((tq, LANES), F32) if diag else rs_ref[c] for c in chains]
            zc, nb = [], []
            for idx, (c, _) in enumerate(work):
                n_c = jnp.maximum(z[idx], 0.0) + jnp.log2(1.0 + jnp.exp2(-jnp.abs(z[idx])))
                if diag:
                    n_c = jnp.where(past, n_c, 0.0)
                nb.append(n_c.astype(BF16))
                zc.append(z[idx] + jnp.tile(rowsum[c], (1, tq // LANES)))
                rowsum[c] = rowsum[c] - jnp.sum(n_c, axis=-1, keepdims=True)

            den = [jnp.zeros((tq, LANES), F32) if diag else den_ref[c] for c in chains]
            p = []
            for idx, (c, _) in enumerate(work):
                p_c = jnp.exp2(y[idx] - shift)
                if diag:
                    p_c = jnp.where(causal, p_c, 0.0)
                den[c] = den[c] + jnp.sum(p_c, axis=-1, keepdims=True)
                p.append(p_c.astype(BF16))

            cum = [_dot(nb[idx], neg_suffix) for idx in range(len(work))]
            dacc = [None if diag else dacc_ref[c] for c in chains]
            for idx, (c, c0) in enumerate(work):
                pv = _dot(p[idx], dv_ref[pl.ds(c0, tq), gsl[c // 2]])
                dacc[c] = pv if dacc[c] is None else dacc[c] + pv

            w = []
            for idx in range(len(work)):
                w_c = jnp.exp2(zc[idx] + cum[idx])
                if diag:
                    w_c = jnp.where(past, w_c, 0.0)
                w.append(w_c.astype(BF16))
            sacc = [None if diag else sacc_ref[c] for c in chains]
            for idx, (c, c0) in enumerate(work):
                pv = _dot(w[idx], sv_ref[pl.ds(c0, tq), gsl[c // 2]])
                sacc[c] = pv if sacc[c] is None else sacc[c] + pv

            for c in chains:
                rs_ref[c] = rowsum[c]
                sacc_ref[c] = sacc[c]
                den_ref[c] = den[c]
                dacc_ref[c] = dacc[c]

        tiles([r0], True)

        @pl.when(i % 2 == 1)
        def _():
            tiles([pl.multiple_of(r0 - tq, tq)], False)

        def two_tiles(jj, _):
            right = pl.multiple_of(r0 - (i % 2) * tq - (2 * jj + 1) * tq, tq)
            tiles([right, pl.multiple_of(right - tq, tq)], False)
            return 0

        lax.fori_loop(0, i // 2, two_tiles, 0)
        for g in range(groups):
            so_ref[pl.ds(r0, tq), gsl[g]] = jnp.where(
                lane_o < HEAD_DIM, sacc_ref[2 * g], sacc_ref[2 * g + 1]).astype(so_ref.dtype)
            o = dacc_ref[2 * g] / den_ref[2 * g] - lam * (dacc_ref[2 * g + 1] / den_ref[2 * g + 1])
            ms = jnp.mean(o * o, axis=-1, keepdims=True)
            o = o * lax.rsqrt(ms + SUBLN_EPS) * gs_ref[...]
            do_ref[pl.ds(r0, tq), gsl[g]] = (o * (1.0 - lambda_init)).astype(do_ref.dtype)
        return 0

    lax.fori_loop(0, nq, q_block, 0)


def _attention(shift, proj, lq1, lk1, lq2, lk2, gs, *, n_pairs, n_heads, tq, groups, lambda_init):
    b, seq, _ = proj.shape
    assert n_pairs == n_heads and n_pairs % groups == 0
    width = groups * LANES
    steps = n_pairs // groups
    chains = 2 * groups
    kern = functools.partial(_attn_kernel, tq=tq, groups=groups, lambda_init=lambda_init)
    spec = lambda part: pl.BlockSpec((None, seq, width), lambda bi, hp: (bi, 0, part * steps + hp))
    small = lambda n: pl.BlockSpec((1, n), lambda bi, hp: (0, 0))
    out_spec = pl.BlockSpec((None, seq, width), lambda bi, hp: (bi, 0, hp))
    out_sds = jax.ShapeDtypeStruct((b, seq, n_pairs * LANES), BF16)
    keys = pltpu.VMEM((chains, seq, LANES), BF16)
    accum = pltpu.VMEM((chains, tq, LANES), F32)
    return pl.pallas_call(
        kern,
        grid=(b, steps),
        in_specs=[pl.BlockSpec(memory_space=pltpu.SMEM)] + [spec(part) for part in range(6)]
        + [small(HEAD_DIM), small(HEAD_DIM), small(HEAD_DIM), small(HEAD_DIM), small(LANES)],
        out_specs=[out_spec, out_spec],
        out_shape=[out_sds, out_sds],
        scratch_shapes=[keys, keys, accum, accum, accum, accum],
        compiler_params=pltpu.CompilerParams(
            dimension_semantics=("arbitrary", "arbitrary"), vmem_limit_bytes=VMEM_LIMIT),
        name="attn",
    )(shift, proj, proj, proj, proj, proj, proj, lq1, lk1, lq2, lk2, gs)


def _out_ffn_kernel(x_ref, sb_ref, df_ref, wo_sb_ref, wo_df_ref, g_ref, wg_ref, wu_ref, wd_ref, o_ref):
    x1 = x_ref[...] + _dot(sb_ref[...], wo_sb_ref[...]) + _dot(df_ref[...], wo_df_ref[...])
    ms = jnp.mean(x1 * x1, axis=-1, keepdims=True)
    h2 = (x1 * lax.rsqrt(ms + NORM_EPS) * g_ref[...]).astype(BF16)
    gate = _dot(h2, wg_ref[...])
    up = _dot(h2, wu_ref[...])
    act = (gate * jax.nn.sigmoid(gate) * up).astype(BF16)
    o_ref[...] = x1 + _dot(act, wd_ref[...])


def _out_ffn(x2, sb, df, wo_sb, wo_df, g, wg, wu, wd, *, tm):
    t, d = x2.shape
    d_ff = wg.shape[1]
    resident = lambda shape: pl.BlockSpec(shape, lambda i: (0, 0), pipeline_mode=pl.Buffered(1))
    return pl.pallas_call(
        _out_ffn_kernel,
        grid=(t // tm,),
        in_specs=[
            pl.BlockSpec((tm, d), lambda i: (i, 0)),
            pl.BlockSpec((tm, sb.shape[1]), lambda i: (i, 0)),
            pl.BlockSpec((tm, df.shape[1]), lambda i: (i, 0)),
            resident(wo_sb.shape), resident(wo_df.shape),
            pl.BlockSpec((1, d), lambda i: (0, 0)),
            resident((d, d_ff)), resident((d, d_ff)), resident((d_ff, d)),
        ],
        out_specs=pl.BlockSpec((tm, d), lambda i: (i, 0)),
        out_shape=jax.ShapeDtypeStruct((t, d), F32),
        compiler_params=pltpu.CompilerParams(
            dimension_semantics=("arbitrary",), vmem_limit_bytes=VMEM_LIMIT),
        name="out_ffn",
    )(x2, sb, df, wo_sb, wo_df, g, wg, wu, wd)


def _rope_tables(seq):
    half = HEAD_DIM // 2
    inv_freq = ROPE_THETA ** (-jnp.arange(half, dtype=F32) / half)
    ang = jnp.arange(seq, dtype=jnp.int32).astype(F32)[:, None] * inv_freq[None, :]
    cos, sin = jnp.cos(ang), jnp.sin(ang)
    reps = LANES // HEAD_DIM
    cos_t = jnp.tile(jnp.concatenate([cos, cos], axis=-1), (1, reps))
    sin_t = jnp.tile(jnp.concatenate([-sin, sin], axis=-1), (1, reps))
    return cos_t, sin_t


def kernel(x, attn_norm_g, w_in, diff_q_norm_g, diff_k_norm_g, lambda_q1, lambda_k1, lambda_q2, lambda_k2,
           diff_subln_g, w_o, ffn_norm_g, w_gate, w_up, w_down):
    b, seq, d = x.shape
    depth = w_in.shape[0]
    mix_width = w_o.shape[1]
    sb_width = mix_width // 2
    diff_width = mix_width - sb_width
    assert sb_width % LANES == 0 and diff_width % LANES == 0
    n_pairs = sb_width // LANES
    n_dheads = diff_width // LANES
    cos_t, sin_t = _rope_tables(seq)
    reps = LANES // HEAD_DIM

    x2 = x.reshape(b * seq, d)
    for layer in range(depth):
        lambda_init = 0.8 - 0.6 * math.exp(-0.3 * layer)
        proj = _inproj(
            x2, attn_norm_g[layer][None, :], w_in[layer].astype(BF16),
            jnp.tile(diff_q_norm_g[layer], reps)[None, :], jnp.tile(diff_k_norm_g[layer], reps)[None, :],
            cos_t, sin_t, seq=seq, sb_width=sb_width, diff_width=diff_width, tm=512)
        proj = proj.reshape(b, seq, -1)
        d0 = 3 * n_pairs
        shift = (HEAD_DIM * Q_SCALE * 1.02 * jnp.max(jnp.abs(diff_q_norm_g[layer]))
                 * jnp.max(jnp.abs(diff_k_norm_g[layer]))).reshape(1).astype(F32)
        lams = (lambda_q1[layer][None, :], lambda_k1[layer][None, :], lambda_q2[layer][None, :],
                lambda_k2[layer][None, :], diff_subln_g[layer][None, :])

        def fused_attention(proj=proj, shift=shift, lams=lams, lambda_init=lambda_init):
            return tuple(_attention(shift, proj, *lams, n_pairs=n_pairs, n_heads=n_dheads, tq=256,
                                    groups=2, lambda_init=lambda_init))

        def separate_attention(proj=proj, shift=shift, lams=lams, lambda_init=lambda_init):
            return (_sb_attention(proj, n_pairs=n_pairs, q_blk=0, k_blk=n_pairs, v_blk=2 * n_pairs,
                                  tq=256, groups=4),
                    _diff_attention(shift, proj, *lams, n_heads=n_dheads, q_blk=d0, k_blk=d0 + n_dheads,
                                    v_blk=d0 + 2 * n_dheads, tq=256, groups=4, lambda_init=lambda_init))

        sb, df = lax.cond(shift[0] <= MAX_STATIC_SHIFT, fused_attention, separate_attention)
        wo = w_o[layer].astype(BF16)
        x2 = _out_ffn(
            x2, sb.reshape(b * seq, sb_width), df.reshape(b * seq, diff_width),
            wo[:sb_width], wo[sb_width:], ffn_norm_g[layer][None, :],
            w_gate[layer].astype(BF16), w_up[layer].astype(BF16), w_down[layer].astype(BF16),
            tm=512)
    return x2.reshape(b, seq, d)
```

```python
import functools
import math

import jax
import jax.numpy as jnp
from jax import lax
from jax.experimental import pallas as pl
from jax.experimental.pallas import tpu as pltpu

HEAD_DIM = 64
LANES = 128
ROPE_THETA = 10000.0
NORM_EPS = 1e-6
SUBLN_EPS = 1e-5
LOG2E = math.log2(math.e)
Q_SCALE = HEAD_DIM ** -0.5 * LOG2E
NEG_BIG = -1e30
MAX_STATIC_SHIFT = 50.0
VMEM_LIMIT = 56 * 1024 * 1024

F32 = jnp.float32
BF16 = jnp.bfloat16


def _dot(a, b):
    return jnp.dot(a, b, preferred_element_type=F32)


def _dot_nt(a, b):
    return lax.dot_general(a, b, (((1,), (1,)), ((), ())), preferred_element_type=F32)


def _inproj_kernel(x_ref, g_ref, w_ref, gq_ref, gk_ref, cos_ref, sin_ref, o_ref, *, sb_width, diff_width):
    x = x_ref[...]
    ms = jnp.mean(x * x, axis=-1, keepdims=True)
    h = (x * lax.rsqrt(ms + NORM_EPS) * g_ref[...]).astype(BF16)

    tm = x.shape[0]
    lane = lax.broadcasted_iota(jnp.int32, (tm, LANES), 1)
    low_head = lane < HEAD_DIM
    first_half = (lane & (HEAD_DIM - 1)) < (HEAD_DIM // 2)
    cos = cos_ref[...]
    sin = sin_ref[...]

    def norm_rope(p, g, scale):
        sq = p * p
        s_lo = jnp.sum(jnp.where(low_head, sq, 0.0), axis=-1, keepdims=True)
        s_hi = jnp.sum(jnp.where(low_head, 0.0, sq), axis=-1, keepdims=True)
        ms2 = jnp.where(low_head, s_lo, s_hi) * (1.0 / HEAD_DIM)
        pn = p * lax.rsqrt(ms2 + NORM_EPS) * g
        swapped = jnp.where(first_half, pltpu.roll(pn, LANES - HEAD_DIM // 2, 1),
                            pltpu.roll(pn, HEAD_DIM // 2, 1))
        out = pn * cos + swapped * sin
        if scale != 1.0:
            out = out * scale
        return out.astype(BF16)

    q_lo, k_lo, v_lo = 0, sb_width, 2 * sb_width
    dq_lo = 3 * sb_width
    dk_lo = dq_lo + diff_width
    dv_lo = dk_lo + diff_width
    n_out = dv_lo + diff_width
    chunk = 512
    starts = sorted(range(0, n_out, chunk), key=lambda c: not dq_lo <= c < dv_lo)
    for c0 in starts:
        p = _dot(h, w_ref[:, c0:c0 + chunk])
        if c0 < k_lo:
            o_ref[:, c0:c0 + chunk] = (p * Q_SCALE).astype(BF16)
        elif dq_lo <= c0 < dv_lo:
            is_q = c0 < dk_lo
            g = gq_ref[...] if is_q else gk_ref[...]
            for l0 in range(0, chunk, LANES):
                o_ref[:, c0 + l0:c0 + l0 + LANES] = norm_rope(
                    p[:, l0:l0 + LANES], g, Q_SCALE if is_q else 1.0)
        else:
            o_ref[:, c0:c0 + chunk] = p.astype(BF16)


def _inproj(x2, g, w_bf, gq_t, gk_t, cos_t, sin_t, *, seq, sb_width, diff_width, tm):
    t, d = x2.shape
    n_out = w_bf.shape[1]
    assert seq % tm == 0 and t % seq == 0
    n_s = seq // tm
    kern = functools.partial(_inproj_kernel, sb_width=sb_width, diff_width=diff_width)
    return pl.pallas_call(
        kern,
        grid=(t // tm,),
        in_specs=[
            pl.BlockSpec((tm, d), lambda i: (i, 0)),
            pl.BlockSpec((1, d), lambda i: (0, 0)),
            pl.BlockSpec((d, n_out), lambda i: (0, 0)),
            pl.BlockSpec((1, LANES), lambda i: (0, 0)),
            pl.BlockSpec((1, LANES), lambda i: (0, 0)),
            pl.BlockSpec((tm, LANES), lambda i: (i % n_s, 0)),
            pl.BlockSpec((tm, LANES), lambda i: (i % n_s, 0)),
        ],
        out_specs=pl.BlockSpec((tm, n_out), lambda i: (i, 0)),
        out_shape=jax.ShapeDtypeStruct((t, n_out), BF16),
        compiler_params=pltpu.CompilerParams(
            dimension_semantics=("arbitrary",), vmem_limit_bytes=VMEM_LIMIT),
        name="inproj",
    )(x2, g, w_bf, gq_t, gk_t, cos_t, sin_t)


def _masked_keys(k_ref, km_ref, groups):
    seq = k_ref.shape[0]
    lane = lax.broadcasted_iota(jnp.int32, (seq, LANES), 1)
    for g in range(groups):
        k_g = k_ref[:, g * LANES:(g + 1) * LANES]
        zero = jnp.zeros_like(k_g)
        km_ref[2 * g] = jnp.where(lane < HEAD_DIM, k_g, zero)
        km_ref[2 * g + 1] = jnp.where(lane < HEAD_DIM, zero, k_g)


def _sb_kernel(q_ref, k_ref, v_ref, o_ref, km_ref, rs_ref, acc_ref, *, tq, groups):
    seq = q_ref.shape[0]
    nq = seq // tq
    _masked_keys(k_ref, km_ref, groups)

    row = lax.broadcasted_iota(jnp.int32, (tq, tq), 0)
    col = lax.broadcasted_iota(jnp.int32, (tq, tq), 1)
    past = col < row
    neg_suffix = jnp.where(row >= col, -1.0, 0.0).astype(BF16)
    lane_o = lax.broadcasted_iota(jnp.int32, (tq, LANES), 1)
    gsl = [slice(g * LANES, (g + 1) * LANES) for g in range(groups)]
    heads = range(2 * groups)

    def q_block(i, _):
        r0 = pl.multiple_of(i * tq, tq)
        qs = [q_ref[pl.ds(r0, tq), gsl[g]] for g in range(groups)]

        def tiles(cols, diag):
            chains = [(h, c0) for h in heads for c0 in cols]
            z = [_dot_nt(qs[h // 2], km_ref[h, pl.ds(c0, tq), :]) for h, c0 in chains]
            rowsum = [jnp.zeros((tq, 1), F32) if diag else rs_ref[h] for h in heads]
            zc, nb = [], []
            for idx, (h, _) in enumerate(chains):
                n_c = jnp.maximum(z[idx], 0.0) + jnp.log2(1.0 + jnp.exp2(-jnp.abs(z[idx])))
                if diag:
                    n_c = jnp.where(past, n_c, 0.0)
                nb.append(n_c.astype(BF16))
                zc.append(z[idx] + rowsum[h])
                rowsum[h] = rowsum[h] - jnp.sum(n_c, axis=-1, keepdims=True)
            cum = [_dot(nb[idx], neg_suffix) for idx in range(len(chains))]
            w = []
            for idx in range(len(chains)):
                w_c = jnp.exp2(zc[idx] + cum[idx])
                if diag:
                    w_c = jnp.where(past, w_c, 0.0)
                w.append(w_c.astype(BF16))
            acc = [None if diag else acc_ref[h] for h in heads]
            for idx, (h, c0) in enumerate(chains):
                pv = _dot(w[idx], v_ref[pl.ds(c0, tq), gsl[h // 2]])
                acc[h] = pv if acc[h] is None else acc[h] + pv
            for h in heads:
                rs_ref[h] = rowsum[h]
                acc_ref[h] = acc[h]

        tiles([r0], True)

        @pl.when(i % 2 == 1)
        def _():
            tiles([pl.multiple_of(r0 - tq, tq)], False)

        def two_tiles(jj, _):
            right = pl.multiple_of(r0 - (i % 2) * tq - (2 * jj + 1) * tq, tq)
            tiles([right, pl.multiple_of(right - tq, tq)], False)
            return 0

        lax.fori_loop(0, i // 2, two_tiles, 0)
        for g in range(groups):
            o_ref[pl.ds(r0, tq), gsl[g]] = jnp.where(
                lane_o < HEAD_DIM, acc_ref[2 * g], acc_ref[2 * g + 1]).astype(o_ref.dtype)
        return 0

    lax.fori_loop(0, nq, q_block, 0)


def _sb_attention(proj, *, n_pairs, q_blk, k_blk, v_blk, tq, groups):
    b, seq, _ = proj.shape
    width = groups * LANES
    steps = n_pairs // groups
    kern = functools.partial(_sb_kernel, tq=tq, groups=groups)
    spec = lambda off: pl.BlockSpec((None, seq, width), lambda bi, hp: (bi, 0, off // groups + hp))
    return pl.pallas_call(
        kern,
        grid=(b, steps),
        in_specs=[spec(q_blk), spec(k_blk), spec(v_blk)],
        out_specs=pl.BlockSpec((None, seq, width), lambda bi, hp: (bi, 0, hp)),
        out_shape=jax.ShapeDtypeStruct((b, seq, n_pairs * LANES), BF16),
        scratch_shapes=[pltpu.VMEM((2 * groups, seq, LANES), BF16),
                        pltpu.VMEM((2 * groups, tq, 1), F32),
                        pltpu.VMEM((2 * groups, tq, LANES), F32)],
        compiler_params=pltpu.CompilerParams(
            dimension_semantics=("arbitrary", "arbitrary"), vmem_limit_bytes=VMEM_LIMIT),
        name="sb_attn",
    )(proj, proj, proj)


def _diff_kernel(shift_ref, q_ref, k_ref, v_ref, lq1_ref, lk1_ref, lq2_ref, lk2_ref, gs_ref, o_ref, km_ref,
                 den_ref, acc_ref, *, tq, groups, lambda_init):
    seq = q_ref.shape[0]
    nq = seq // tq
    _masked_keys(k_ref, km_ref, groups)

    lam = (jnp.exp(jnp.sum(lq1_ref[...] * lk1_ref[...], axis=-1, keepdims=True))
           - jnp.exp(jnp.sum(lq2_ref[...] * lk2_ref[...], axis=-1, keepdims=True))
           + lambda_init)

    row = lax.broadcasted_iota(jnp.int32, (tq, tq), 0)
    col = lax.broadcasted_iota(jnp.int32, (tq, tq), 1)
    causal = col <= row
    gsl = [slice(g * LANES, (g + 1) * LANES) for g in range(groups)]
    maps = range(2 * groups)

    def finish(r0, g, num1, den1, num2, den2):
        o = num1 / den1 - lam * (num2 / den2)
        ms = jnp.mean(o * o, axis=-1, keepdims=True)
        o = o * lax.rsqrt(ms + SUBLN_EPS) * gs_ref[...]
        o_ref[pl.ds(r0, tq), gsl[g]] = (o * (1.0 - lambda_init)).astype(o_ref.dtype)

    def q_block_static_shift(i, _):
        shift = shift_ref[0]
        r0 = pl.multiple_of(i * tq, tq)
        qs = [q_ref[pl.ds(r0, tq), gsl[g]] for g in range(groups)]

        def tiles(cols, diag):
            chains = [(n, c0) for n in maps for c0 in cols]
            z = [_dot_nt(qs[n // 2], km_ref[n, pl.ds(c0, tq), :]) for n, c0 in chains]
            den = [None if diag else den_ref[n] for n in maps]
            acc = [None if diag else acc_ref[n] for n in maps]
            p = []
            for idx, (n, _) in enumerate(chains):
                p_c = jnp.exp2(z[idx] - shift)
                if diag:
                    p_c = jnp.where(causal, p_c, 0.0)
                p_sum = jnp.sum(p_c, axis=-1, keepdims=True)
                den[n] = p_sum if den[n] is None else den[n] + p_sum
                p.append(p_c.astype(BF16))
            for idx, (n, c0) in enumerate(chains):
                pv = _dot(p[idx], v_ref[pl.ds(c0, tq), gsl[n // 2]])
                acc[n] = pv if acc[n] is None else acc[n] + pv
            for n in maps:
                den_ref[n] = den[n]
                acc_ref[n] = acc[n]

        tiles([r0], True)

        @pl.when(i % 2 == 1)
        def _():
            tiles([pl.multiple_of(r0 - tq, tq)], False)

        def two_tiles(jj, _):
            right = pl.multiple_of(r0 - (i % 2) * tq - (2 * jj + 1) * tq, tq)
            tiles([right, pl.multiple_of(right - tq, tq)], False)
            return 0

        lax.fori_loop(0, i // 2, two_tiles, 0)
        for g in range(groups):
            finish(r0, g, acc_ref[2 * g], den_ref[2 * g], acc_ref[2 * g + 1], den_ref[2 * g + 1])
        return 0

    def q_block_running_max(i, _):
        r0 = pl.multiple_of(i * tq, tq)
        qs = [q_ref[pl.ds(r0, tq), gsl[g]] for g in range(groups)]

        def tiles(c0, carries, diag):
            z = [_dot_nt(qs[n // 2], km_ref[n, pl.ds(c0, tq), :]) for n in maps]
            m_new, alpha, l_new, p = [], [], [], []
            for n in maps:
                m, l, _ = carries[n]
                z_n = jnp.where(causal, z[n], NEG_BIG) if diag else z[n]
                m_n = jnp.maximum(m, jnp.max(z_n, axis=-1, keepdims=True))
                a_n = jnp.exp2(m - m_n)
                p_n = jnp.exp2(z_n - m_n)
                m_new.append(m_n)
                alpha.append(a_n)
                l_new.append(a_n * l + jnp.sum(p_n, axis=-1, keepdims=True))
                p.append(p_n.astype(BF16))
            return tuple(
                (m_new[n], l_new[n],
                 alpha[n] * carries[n][2] + _dot(p[n], v_ref[pl.ds(c0, tq), gsl[n // 2]]))
                for n in maps)

        init = tuple((jnp.full((tq, 1), NEG_BIG, F32), jnp.zeros((tq, 1), F32),
                      jnp.zeros((tq, LANES), F32)) for _ in maps)
        carries = tiles(r0, init, True)
        carries = lax.fori_loop(
            0, i, lambda jj, cs: tiles(pl.multiple_of((i - 1 - jj) * tq, tq), cs, False), carries)
        for g in range(groups):
            (_, l1, a1), (_, l2, a2) = carries[2 * g], carries[2 * g + 1]
            finish(r0, g, a1, l1, a2, l2)
        return 0

    use_static_shift = shift_ref[0] <= MAX_STATIC_SHIFT

    @pl.when(use_static_shift)
    def _():
        lax.fori_loop(0, nq, q_block_static_shift, 0)

    @pl.when(jnp.logical_not(use_static_shift))
    def _():
        lax.fori_loop(0, nq, q_block_running_max, 0)


def _diff_attention(shift, proj, lq1, lk1, lq2, lk2, gs, *, n_heads, q_blk, k_blk, v_blk, tq, groups,
                    lambda_init):
    b, seq, _ = proj.shape
    width = groups * LANES
    steps = n_heads // groups
    kern = functools.partial(_diff_kernel, tq=tq, groups=groups, lambda_init=lambda_init)
    spec = lambda off: pl.BlockSpec((None, seq, width), lambda bi, h: (bi, 0, off // groups + h))
    small = lambda n: pl.BlockSpec((1, n), lambda bi, h: (0, 0))
    return pl.pallas_call(
        kern,
        grid=(b, steps),
        in_specs=[pl.BlockSpec(memory_space=pltpu.SMEM),
                  spec(q_blk), spec(k_blk), spec(v_blk),
                  small(HEAD_DIM), small(HEAD_DIM), small(HEAD_DIM), small(HEAD_DIM), small(LANES)],
        out_specs=pl.BlockSpec((None, seq, width), lambda bi, h: (bi, 0, h)),
        out_shape=jax.ShapeDtypeStruct((b, seq, n_heads * LANES), BF16),
        scratch_shapes=[pltpu.VMEM((2 * groups, seq, LANES), BF16),
                        pltpu.VMEM((2 * groups, tq, 1), F32),
                        pltpu.VMEM((2 * groups, tq, LANES), F32)],
        compiler_params=pltpu.CompilerParams(
            dimension_semantics=("arbitrary", "arbitrary"), vmem_limit_bytes=VMEM_LIMIT),
        name="diff_attn",
    )(shift, proj, proj, proj, lq1, lk1, lq2, lk2, gs)


def _attn_kernel(shift_ref, sq_ref, sk_ref, sv_ref, dq_ref, dk_ref, dv_ref,
                 lq1_ref, lk1_ref, lq2_ref, lk2_ref, gs_ref, so_ref, do_ref,
                 skm_ref, dkm_ref, rs_ref, sacc_ref, den_ref, dacc_ref, *, tq, groups, lambda_init):
    seq = sq_ref.shape[0]
    nq = seq // tq
    _masked_keys(sk_ref, skm_ref, groups)
    _masked_keys(dk_ref, dkm_ref, groups)

    lam = (jnp.exp(jnp.sum(lq1_ref[...] * lk1_ref[...], axis=-1, keepdims=True))
           - jnp.exp(jnp.sum(lq2_ref[...] * lk2_ref[...], axis=-1, keepdims=True))
           + lambda_init)
    shift = shift_ref[0]

    row = lax.broadcasted_iota(jnp.int32, (tq, tq), 0)
    col = lax.broadcasted_iota(jnp.int32, (tq, tq), 1)
    past = col < row
    causal = col <= row
    neg_suffix = jnp.where(row >= col, -1.0, 0.0).astype(BF16)
    lane_o = lax.broadcasted_iota(jnp.int32, (tq, LANES), 1)
    gsl = [slice(g * LANES, (g + 1) * LANES) for g in range(groups)]
    chains = range(2 * groups)

    def finish(rows):
        for g in range(groups):
            so_ref[pl.ds(rows, tq), gsl[g]] = jnp.where(
                lane_o < HEAD_DIM, sacc_ref[2 * g], sacc_ref[2 * g + 1]).astype(so_ref.dtype)
            o = dacc_ref[2 * g] / den_ref[2 * g] - lam * (dacc_ref[2 * g + 1] / den_ref[2 * g + 1])
            ms = jnp.mean(o * o, axis=-1, keepdims=True)
            o = o * lax.rsqrt(ms + SUBLN_EPS) * gs_ref[...]
            do_ref[pl.ds(rows, tq), gsl[g]] = (o * (1.0 - lambda_init)).astype(do_ref.dtype)

    def q_block(i, _, finish_previous=True):
        r0 = i * tq if isinstance(i, int) else pl.multiple_of(i * tq, tq)
        sqs = [sq_ref[pl.ds(r0, tq), gsl[g]] for g in range(groups)]
        dqs = [dq_ref[pl.ds(r0, tq), gsl[g]] for g in range(groups)]

        def tiles(cols, diag):
            work = [(c, c0) for c in chains for c0 in cols]
            z = [_dot_nt(sqs[c // 2], skm_ref[c, pl.ds(c0, tq), :]) for c, c0 in work]
            y = [_dot_nt(dqs[c // 2], dkm_ref[c, pl.ds(c0, tq), :]) for c, c0 in work]

            rowsum = [jnp.zeros((tq, LANES), F32) if diag else rs_ref[c] for c in chains]
            zc, nb = [], []
            for idx, (c, _) in enumerate(work):
                n_c = jnp.maximum(z[idx], 0.0) + jnp.log2(1.0 + jnp.exp2(-jnp.abs(z[idx])))
                if diag:
                    n_c = jnp.where(past, n_c, 0.0)
                nb.append(n_c.astype(BF16))
                zc.append(z[idx] + jnp.tile(rowsum[c], (1, tq // LANES)))
                rowsum[c] = rowsum[c] - jnp.sum(n_c, axis=-1, keepdims=True)

            den = [jnp.zeros((tq, LANES), F32) if diag else den_ref[c] for c in chains]
            p = []
            for idx, (c, _) in enumerate(work):
                p_c = jnp.exp2(y[idx] - shift)
                if diag:
                    p_c = jnp.where(causal, p_c, 0.0)
                den[c] = den[c] + jnp.sum(p_c, axis=-1, keepdims=True)
                p.append(p_c.astype(BF16))

            dacc = [None if diag else dacc_ref[c] for c in chains]
            for idx, (c, c0) in enumerate(work):
                pv = _dot(p[idx], dv_ref[pl.ds(c0, tq), gsl[c // 2]])
                dacc[c] = pv if dacc[c] is None else dacc[c] + pv
            cum = [_dot(nb[idx], neg_suffix) for idx in range(len(work))]

            w = []
            for idx in range(len(work)):
                w_c = jnp.exp2(zc[idx] + cum[idx])
                if diag:
                    w_c = jnp.where(past, w_c, 0.0)
                w.append(w_c.astype(BF16))
            sacc = [None if diag else sacc_ref[c] for c in chains]
            for idx, (c, c0) in enumerate(work):
                pv = _dot(w[idx], sv_ref[pl.ds(c0, tq), gsl[c // 2]])
                sacc[c] = pv if sacc[c] is None else sacc[c] + pv

            for c in chains:
                rs_ref[c] = rowsum[c]
                sacc_ref[c] = sacc[c]
                den_ref[c] = den[c]
                dacc_ref[c] = dacc[c]

        if finish_previous:
            finish(pl.multiple_of(r0 - tq, tq))
        tiles([r0], True)

        @pl.when(i % 2 == 1)
        def _():
            tiles([pl.multiple_of(r0 - tq, tq)], False)

        def two_tiles(jj, _):
            right = pl.multiple_of(r0 - (i % 2) * tq - (2 * jj + 1) * tq, tq)
            tiles([right, pl.multiple_of(right - tq, tq)], False)
            return 0

        lax.fori_loop(0, i // 2, two_tiles, 0)
        return 0

    q_block(0, 0, finish_previous=False)
    lax.fori_loop(1, nq, q_block, 0)
    finish((nq - 1) * tq)


def _attention(shift, proj, lq1, lk1, lq2, lk2, gs, *, n_pairs, n_heads, tq, groups, lambda_init):
    b, seq, _ = proj.shape
    assert n_pairs == n_heads and n_pairs % groups == 0
    width = groups * LANES
    steps = n_pairs // groups
    chains = 2 * groups
    kern = functools.partial(_attn_kernel, tq=tq, groups=groups, lambda_init=lambda_init)
    spec = lambda part: pl.BlockSpec((None, seq, width), lambda bi, hp: (bi, 0, part * steps + hp))
    small = lambda n: pl.BlockSpec((1, n), lambda bi, hp: (0, 0))
    out_spec = pl.BlockSpec((None, seq, width), lambda bi, hp: (bi, 0, hp))
    out_sds = jax.ShapeDtypeStruct((b, seq, n_pairs * LANES), BF16)
    keys = pltpu.VMEM((chains, seq, LANES), BF16)
    accum = pltpu.VMEM((chains, tq, LANES), F32)
    return pl.pallas_call(
        kern,
        grid=(b, steps),
        in_specs=[pl.BlockSpec(memory_space=pltpu.SMEM)] + [spec(part) for part in range(6)]
        + [small(HEAD_DIM), small(HEAD_DIM), small(HEAD_DIM), small(HEAD_DIM), small(LANES)],
        out_specs=[out_spec, out_spec],
        out_shape=[out_sds, out_sds],
        scratch_shapes=[keys, keys, accum, accum, accum, accum],
        compiler_params=pltpu.CompilerParams(
            dimension_semantics=("arbitrary", "arbitrary"), vmem_limit_bytes=VMEM_LIMIT),
        name="attn",
    )(shift, proj, proj, proj, proj, proj, proj, lq1, lk1, lq2, lk2, gs)


def _out_ffn_kernel(x_ref, sb_ref, df_ref, wo_sb_ref, wo_df_ref, g_ref, wg_ref, wu_ref, wd_ref, o_ref):
    x1 = x_ref[...] + _dot(sb_ref[...], wo_sb_ref[...]) + _dot(df_ref[...], wo_df_ref[...])
    ms = jnp.mean(x1 * x1, axis=-1, keepdims=True)
    h2 = (x1 * lax.rsqrt(ms + NORM_EPS) * g_ref[...]).astype(BF16)
    gate = _dot(h2, wg_ref[...])
    up = _dot(h2, wu_ref[...])
    act = (gate * jax.nn.sigmoid(gate) * up).astype(BF16)
    o_ref[...] = x1 + _dot(act, wd_ref[...])


def _out_ffn(x2, sb, df, wo_sb, wo_df, g, wg, wu, wd, *, tm):
    t, d = x2.shape
    d_ff = wg.shape[1]
    resident = lambda shape: pl.BlockSpec(shape, lambda i: (0, 0), pipeline_mode=pl.Buffered(1))
    return pl.pallas_call(
        _out_ffn_kernel,
        grid=(t // tm,),
        in_specs=[
            pl.BlockSpec((tm, d), lambda i: (i, 0)),
            pl.BlockSpec((tm, sb.shape[1]), lambda i: (i, 0)),
            pl.BlockSpec((tm, df.shape[1]), lambda i: (i, 0)),
            resident(wo_sb.shape), resident(wo_df.shape),
            pl.BlockSpec((1, d), lambda i: (0, 0)),
            resident((d, d_ff)), resident((d, d_ff)), resident((d_ff, d)),
        ],
        out_specs=pl.BlockSpec((tm, d), lambda i: (i, 0)),
        out_shape=jax.ShapeDtypeStruct((t, d), F32),
        compiler_params=pltpu.CompilerParams(
            dimension_semantics=("arbitrary",), vmem_limit_bytes=VMEM_LIMIT),
        name="out_ffn",
    )(x2, sb, df, wo_sb, wo_df, g, wg, wu, wd)


def _rope_tables(seq):
    half = HEAD_DIM // 2
    inv_freq = ROPE_THETA ** (-jnp.arange(half, dtype=F32) / half)
    ang = jnp.arange(seq, dtype=jnp.int32).astype(F32)[:, None] * inv_freq[None, :]
    cos, sin = jnp.cos(ang), jnp.sin(ang)
    reps = LANES // HEAD_DIM
    cos_t = jnp.tile(jnp.concatenate([cos, cos], axis=-1), (1, reps))
    sin_t = jnp.tile(jnp.concatenate([-sin, sin], axis=-1), (1, reps))
    return cos_t, sin_t


def kernel(x, attn_norm_g, w_in, diff_q_norm_g, diff_k_norm_g, lambda_q1, lambda_k1, lambda_q2, lambda_k2,
           diff_subln_g, w_o, ffn_norm_g, w_gate, w_up, w_down):
    b, seq, d = x.shape
    depth = w_in.shape[0]
    mix_width = w_o.shape[1]
    sb_width = mix_width // 2
    diff_width = mix_width - sb_width
    assert sb_width % LANES == 0 and diff_width % LANES == 0
    n_pairs = sb_width // LANES
    n_dheads = diff_width // LANES
    cos_t, sin_t = _rope_tables(seq)
    reps = LANES // HEAD_DIM

    x2 = x.reshape(b * seq, d)
    for layer in range(depth):
        lambda_init = 0.8 - 0.6 * math.exp(-0.3 * layer)
        proj = _inproj(
            x2, attn_norm_g[layer][None, :], w_in[layer].astype(BF16),
            jnp.tile(diff_q_norm_g[layer], reps)[None, :], jnp.tile(diff_k_norm_g[layer], reps)[None, :],
            cos_t, sin_t, seq=seq, sb_width=sb_width, diff_width=diff_width, tm=1024)
        proj = proj.reshape(b, seq, -1)
        d0 = 3 * n_pairs
        shift = (HEAD_DIM * Q_SCALE * 1.02 * jnp.max(jnp.abs(diff_q_norm_g[layer]))
                 * jnp.max(jnp.abs(diff_k_norm_g[layer]))).reshape(1).astype(F32)
        lams = (lambda_q1[layer][None, :], lambda_k1[layer][None, :], lambda_q2[layer][None, :],
                lambda_k2[layer][None, :], diff_subln_g[layer][None, :])

        def fused_attention(proj=proj, shift=shift, lams=lams, lambda_init=lambda_init):
            return tuple(_attention(shift, proj, *lams, n_pairs=n_pairs, n_heads=n_dheads, tq=256,
                                    groups=2, lambda_init=lambda_init))

        def separate_attention(proj=proj, shift=shift, lams=lams, lambda_init=lambda_init):
            return (_sb_attention(proj, n_pairs=n_pairs, q_blk=0, k_blk=n_pairs, v_blk=2 * n_pairs,
                                  tq=256, groups=4),
                    _diff_attention(shift, proj, *lams, n_heads=n_dheads, q_blk=d0, k_blk=d0 + n_dheads,
                                    v_blk=d0 + 2 * n_dheads, tq=256, groups=4, lambda_init=lambda_init))

        sb, df = lax.cond(shift[0] <= MAX_STATIC_SHIFT, fused_attention, separate_attention)
        wo = w_o[layer].astype(BF16)
        x2 = _out_ffn(
            x2, sb.reshape(b * seq, sb_width), df.reshape(b * seq, diff_width),
            wo[:sb_width], wo[sb_width:], ffn_norm_g[layer][None, :],
            w_gate[layer].astype(BF16), w_up[layer].astype(BF16), w_down[layer].astype(BF16),
            tm=512)
    return x2.reshape(b, seq, d)
```
